```python
import math
import jax
import jax.numpy as jnp
from jax import lax
import numpy as np

D_MODEL = 4096
BATCH = 4
SEQ = 2048
DEPTH = 2

GRID_W = 64
CTX_LEN = 256
HEAD_DIM = 128
MIX_WIDTH = D_MODEL
N_HEAD_SLOTS = MIX_WIDTH // HEAD_DIM
A_HEADS = N_HEAD_SLOTS // 4
B_HEADS = (N_HEAD_SLOTS - A_HEADS) // 2
C_HEADS = N_HEAD_SLOTS - A_HEADS - B_HEADS
A_WIDTH = A_HEADS * HEAD_DIM
CHUNK = 128
Q_LORA = 1536
KV_LORA = 512
QK_NOPE = 128
QK_ROPE = 64
V_DIM = HEAD_DIM
B_QK = QK_NOPE + QK_ROPE
C_KV_HEADS = C_HEADS // 3
C_GROUP = C_HEADS // C_KV_HEADS
WINDOW = 128
BLOCK = 128
PEER_HEADS = 8
N_KEYS = 128
N_EXPERTS = N_KEYS * N_KEYS
PEER_TOPK = 16
PEER_QDIM = 256
PEER_HALF = PEER_QDIM // 2
PEER_BLOCK = 64
ROPE_BASE = 10000.0
EPS = 1e-6
NEG = -1e30
N_MOD = 6
P_A = 2 * A_WIDTH
P_B = Q_LORA + KV_LORA + QK_ROPE
P_C = (C_HEADS + 2 * C_KV_HEADS) * HEAD_DIM
P_IN = P_A + P_B + P_C

kernel_name = 'hybrid_prefix_dit_block'


def rms_norm(x, gain):
    xf = x.astype(jnp.float32)
    y = xf * lax.rsqrt(jnp.mean(xf * xf, axis=-1, keepdims=True) + EPS)
    return (y * gain.astype(jnp.float32)).astype(x.dtype)


def modulate(h, shift, scale):
    return h * (1 + scale) + shift


def axial_rope_tables(row, col, rot_dim):
    n_freq = rot_dim // 4
    inv = ROPE_BASE ** (-jnp.arange(n_freq, dtype=jnp.float32) / n_freq)
    ar = row[:, None] * inv
    ac = col[:, None] * inv
    ang = jnp.concatenate([ar, ar, ac, ac], axis=-1)
    return jnp.cos(ang), jnp.sin(ang)


def apply_rope(x, cos, sin):
    xf = x.astype(jnp.float32)
    xs = xf.reshape(xf.shape[:-1] + (2, 2, -1))
    rot = jnp.concatenate([-xs[..., 1:, :], xs[..., :1, :]], axis=-2).reshape(xf.shape)
    return (xf * cos[:, None, :] + rot * sin[:, None, :]).astype(x.dtype)


def split_groups(p):
    return p[..., :P_A], p[..., P_A:P_A + P_B], p[..., P_A + P_B:]


def chunk_gmlp(p, v_gain, w_s, b_s):
    b, l, _ = p.shape
    z = jax.nn.gelu(p)
    u, v = z[..., :A_WIDTH], z[..., A_WIDTH:]
    v = rms_norm(v.reshape(b, l, A_HEADS, HEAD_DIM), v_gain.reshape(A_HEADS, HEAD_DIM))
    v = v.reshape(b, l // CHUNK, CHUNK, A_WIDTH)
    s = jnp.einsum('ts,bnsc->bntc', w_s, v) + b_s[:, None]
    return u * s.reshape(b, l, A_WIDTH)


def softmax_attend(q, k, v):
    s = jnp.einsum('bhqd,bhkd->bhqk', q, k).astype(jnp.float32) * (q.shape[-1] ** -0.5)
    p = jax.nn.softmax(s, axis=-1).astype(v.dtype)
    return jnp.einsum('bhqk,bhkd->bhqd', p, v)


def attend_blocked(q, k, v):
    b, h, l, d = q.shape
    qb = jnp.moveaxis(q.reshape(b, h, l // BLOCK, BLOCK, d), 2, 0)
    ob = lax.map(lambda qi: softmax_attend(qi, k, v), qb)
    return jnp.moveaxis(ob, 0, 2).reshape(b, h, l, -1)


def heads_to_tokens(o):
    b, h, l, d = o.shape
    return o.transpose(0, 2, 1, 3).reshape(b, l, h * d)


def sink_softmax(s, sink):
    sk = jnp.broadcast_to(sink, s.shape[:-1] + (1,))
    return jax.nn.softmax(jnp.concatenate([s, sk], axis=-1), axis=-1)[..., :-1]


def mla_qkv(p, q_gain, kv_gain, w_uq, w_ukv, qn_gain, kn_gain, rope):
    b, l, _ = p.shape
    c_q = p[..., :Q_LORA]
    c_kv = p[..., Q_LORA:Q_LORA + KV_LORA]
    k_r = p[..., Q_LORA + KV_LORA:]
    q = (rms_norm(c_q, q_gain) @ w_uq).reshape(b, l, B_HEADS, B_QK)
    kv = (rms_norm(c_kv, kv_gain) @ w_ukv).reshape(b, l, B_HEADS, QK_NOPE + V_DIM)
    k = jnp.concatenate([kv[..., :QK_NOPE],
                         jnp.broadcast_to(k_r[:, :, None, :], (b, l, B_HEADS, QK_ROPE))], axis=-1)
    v = kv[..., QK_NOPE:]
    q = rms_norm(q, qn_gain)
    k = rms_norm(k, kn_gain)
    if rope is not None:
        cos, sin = rope
        q = jnp.concatenate([q[..., :QK_NOPE], apply_rope(q[..., QK_NOPE:], cos, sin)], axis=-1)
        k = jnp.concatenate([k[..., :QK_NOPE], apply_rope(k[..., QK_NOPE:], cos, sin)], axis=-1)
    return q.transpose(0, 2, 1, 3), k.transpose(0, 2, 1, 3), v.transpose(0, 2, 1, 3)


def gqa_qkv(p, qn_gain, kn_gain, rope):
    b, l, _ = p.shape
    nq = C_HEADS * HEAD_DIM
    nk = C_KV_HEADS * HEAD_DIM
    q = rms_norm(p[..., :nq].reshape(b, l, C_HEADS, HEAD_DIM), qn_gain)
    k = rms_norm(p[..., nq:nq + nk].reshape(b, l, C_KV_HEADS, HEAD_DIM), kn_gain)
    v = p[..., nq + nk:].reshape(b, l, C_KV_HEADS, HEAD_DIM)
    if rope is not None:
        cos, sin = rope
        q = apply_rope(q, cos, sin)
        k = apply_rope(k, cos, sin)
    q = q.reshape(b, l, C_KV_HEADS, C_GROUP, HEAD_DIM).transpose(0, 2, 3, 1, 4)
    return q, k.transpose(0, 2, 1, 3), v.transpose(0, 2, 1, 3)


def window_attention_with_ctx(q, k, v, k_ctx, v_ctx, sink):
    b, hk, g, l, d = q.shape
    nb = l // BLOCK
    scale = d ** -0.5
    qb = q.reshape(b, hk, g, nb, BLOCK, d)

    def band(t):
        tb = t.reshape(b, hk, nb, BLOCK, -1)
        tp = jnp.pad(tb, ((0, 0), (0, 0), (1, 1), (0, 0), (0, 0)))
        return jnp.concatenate([tp[:, :, :-2], tp[:, :, 1:-1], tp[:, :, 2:]], axis=3)

    kb, vb = band(k), band(v)
    s_band = jnp.einsum('bkgnqd,bknjd->bkgnqj', qb, kb).astype(jnp.float32) * scale
    s_ctx = jnp.einsum('bkgnqd,bkjd->bkgnqj', qb, k_ctx).astype(jnp.float32) * scale
    blk = jnp.arange(nb)[:, None, None]
    qpos = blk * BLOCK + jnp.arange(BLOCK)[None, :, None]
    kpos = (blk - 1) * BLOCK + jnp.arange(3 * BLOCK)[None, None, :]
    valid = (jnp.abs(qpos - kpos) <= WINDOW) & (kpos >= 0) & (kpos < l)
    s_band = jnp.where(valid, s_band, NEG)
    sink_b = sink.astype(jnp.float32).reshape(1, hk, g, 1, 1, 1)
    p = sink_softmax(jnp.concatenate([s_ctx, s_band], axis=-1), sink_b)
    n_ctx = k_ctx.shape[2]
    o = (jnp.einsum('bkgnqj,bkjd->bkgnqd', p[..., :n_ctx].astype(v.dtype), v_ctx)
         + jnp.einsum('bkgnqj,bknjd->bkgnqd', p[..., n_ctx:].astype(v.dtype), vb))
    return o.reshape(b, hk, g, l, d)


def context_sink_attention(q, k, v, sink):
    hk, g = q.shape[1], q.shape[2]
    s = jnp.einsum('bkgqd,bkjd->bkgqj', q, k).astype(jnp.float32) * (q.shape[-1] ** -0.5)
    p = sink_softmax(s, sink.astype(jnp.float32).reshape(1, hk, g, 1, 1))
    return jnp.einsum('bkgqj,bkjd->bkgqd', p.astype(v.dtype), v)


def gqa_out(o):
    b, hk, g, l, d = o.shape
    return o.transpose(0, 3, 1, 2, 4).reshape(b, l, hk * g * d)


def peer_ffn(h, w_q, subkeys, u_tab, v_tab):
    b, l, dm = h.shape
    hb = h.reshape(-1, PEER_BLOCK, dm)

    def one(t):
        q = (t @ w_q).reshape(PEER_BLOCK, PEER_HEADS, 2, PEER_HALF)
        s = jnp.einsum('thpd,hpnd->thpn', q, subkeys).astype(jnp.float32)
        top_s, top_i = lax.top_k(s, PEER_TOPK)
        cand_s = (top_s[:, :, 0, :, None] + top_s[:, :, 1, None, :]).reshape(PEER_BLOCK, PEER_HEADS, -1)
        cand_i = (top_i[:, :, 0, :, None] * N_KEYS + top_i[:, :, 1, None, :]).reshape(PEER_BLOCK, PEER_HEADS, -1)
        best_s, best_j = lax.top_k(cand_s, PEER_TOPK)
        idx = jnp.take_along_axis(cand_i, best_j, axis=-1)
        gate = jax.nn.softmax(best_s, axis=-1)
        u = jnp.take(u_tab, idx, axis=0)
        act = jax.nn.gelu(jnp.einsum('td,thkd->thk', t, u).astype(jnp.float32))
        w = (gate * act).astype(t.dtype)
        v = jnp.take(v_tab, idx, axis=0)
        return jnp.einsum('thk,thkd->td', w, v)

    return lax.map(one, hb).reshape(b, l, dm)


def setup_inputs(seed: int = 0) -> dict:
    key = jax.random.key(seed)
    ks = jax.random.split(key, 32)
    f32 = jnp.float32
    L, D = DEPTH, D_MODEL

    def nrm(k, shape, scale):
        return jax.random.normal(k, shape, f32) * scale

    def gain(k, shape):
        return 1.0 + 0.02 * jax.random.normal(k, shape, f32)

    return {
        'x': nrm(ks[0], (BATCH, SEQ, D), 1.0),
        'c': nrm(ks[1], (BATCH, D), 1.0),
        'ctx': nrm(ks[2], (BATCH, CTX_LEN, D), 1.0),
        'c_ctx': nrm(ks[3], (D,), 1.0),
        'w_mod': nrm(ks[4], (L, D, N_MOD * D), 0.5 * D ** -0.5),
        'b_mod': nrm(ks[5], (L, N_MOD * D), 0.02),
        'norm1_gain': gain(ks[6], (L, D)),
        'norm2_gain': gain(ks[7], (L, D)),
        'w_in': nrm(ks[8], (L, D, P_IN), D ** -0.5),
        'a_v_gain': gain(ks[9], (L, A_WIDTH)),
        'a_w_s': nrm(ks[10], (L, CHUNK, CHUNK), CHUNK ** -0.5),
        'a_b_s': gain(ks[11], (L, CHUNK)),
        'b_q_gain': gain(ks[12], (L, Q_LORA)),
        'b_kv_gain': gain(ks[13], (L, KV_LORA)),
        'b_w_uq': nrm(ks[14], (L, Q_LORA, B_HEADS * B_QK), Q_LORA ** -0.5),
        'b_w_ukv': nrm(ks[15], (L, KV_LORA, B_HEADS * (QK_NOPE + V_DIM)), KV_LORA ** -0.5),
        'b_qn_gain': gain(ks[16], (L, B_QK)),
        'b_kn_gain': gain(ks[17], (L, B_QK)),
        'c_qn_gain': gain(ks[18], (L, HEAD_DIM)),
        'c_kn_gain': gain(ks[19], (L, HEAD_DIM)),
        'c_sink': nrm(ks[20], (L, C_HEADS), 0.5),
        'w_out': nrm(ks[21], (L, MIX_WIDTH, D), MIX_WIDTH ** -0.5),
        'peer_w_q': nrm(ks[22], (L, D, PEER_HEADS * PEER_QDIM), D ** -0.5),
        'peer_subkeys': nrm(ks[23], (L, PEER_HEADS, 2, N_KEYS, PEER_HALF), PEER_HALF ** -0.5),
        'peer_u': nrm(ks[24], (L, N_EXPERTS, D), D ** -0.5),
        'peer_v': nrm(ks[25], (L, N_EXPERTS, D), PEER_HEADS ** -0.5),
    }


def reference(x, c, ctx, c_ctx, w_mod, b_mod, norm1_gain, norm2_gain, w_in, a_v_gain, a_w_s, a_b_s,
              b_q_gain, b_kv_gain, b_w_uq, b_w_ukv, b_qn_gain, b_kn_gain, c_qn_gain, c_kn_gain, c_sink,
              w_out, peer_w_q, peer_subkeys, peer_u, peer_v):
    s_len = x.shape[1]
    rows = s_len // GRID_W
    row = jnp.repeat(jnp.arange(rows, dtype=jnp.float32), GRID_W)
    col = jnp.tile(jnp.arange(GRID_W, dtype=jnp.float32), rows)
    rope_b = axial_rope_tables(row, col, QK_ROPE)
    rope_c = axial_rope_tables(row, col, HEAD_DIM)
    silu_c = jax.nn.silu(c)
    silu_cc = jax.nn.silu(c_ctx)

    for layer in range(DEPTH):
        need_ctx = layer < DEPTH - 1
        m_lat = jnp.split((silu_c @ w_mod[layer] + b_mod[layer])[:, None, :], N_MOD, axis=-1)
        m_ctx = jnp.split(silu_cc @ w_mod[layer] + b_mod[layer], N_MOD, axis=-1)

        h_lat = modulate(rms_norm(x, norm1_gain[layer]), m_lat[0], m_lat[1])
        h_ctx = modulate(rms_norm(ctx, norm1_gain[layer]), m_ctx[0], m_ctx[1])
        pa_lat, pb_lat, pc_lat = split_groups(h_lat @ w_in[layer])
        pa_ctx, pb_ctx, pc_ctx = split_groups(h_ctx @ w_in[layer])

        oa_lat = chunk_gmlp(pa_lat, a_v_gain[layer], a_w_s[layer], a_b_s[layer])

        qb_l, kb_l, vb_l = mla_qkv(pb_lat, b_q_gain[layer], b_kv_gain[layer], b_w_uq[layer], b_w_ukv[layer],
                                   b_qn_gain[layer], b_kn_gain[layer], rope_b)
        qb_c, kb_c, vb_c = mla_qkv(pb_ctx, b_q_gain[layer], b_kv_gain[layer], b_w_uq[layer], b_w_ukv[layer],
                                   b_qn_gain[layer], b_kn_gain[layer], None)
        ob_lat = attend_blocked(qb_l, jnp.concatenate([kb_c, kb_l], axis=2), jnp.concatenate([vb_c, vb_l], axis=2))

        qc_l, kc_l, vc_l = gqa_qkv(pc_lat, c_qn_gain[layer], c_kn_gain[layer], rope_c)
        qc_c, kc_c, vc_c = gqa_qkv(pc_ctx, c_qn_gain[layer], c_kn_gain[layer], None)
        oc_lat = window_attention_with_ctx(qc_l, kc_l, vc_l, kc_c, vc_c, c_sink[layer])

        o_lat = jnp.concatenate([oa_lat, heads_to_tokens(ob_lat), gqa_out(oc_lat)], axis=-1)
        x = x + m_lat[2] * (o_lat @ w_out[layer])
        if need_ctx:
            oa_ctx = chunk_gmlp(pa_ctx, a_v_gain[layer], a_w_s[layer], a_b_s[layer])
            ob_ctx = softmax_attend(qb_c, kb_c, vb_c)
            oc_ctx = context_sink_attention(qc_c, kc_c, vc_c, c_sink[layer])
            o_ctx = jnp.concatenate([oa_ctx, heads_to_tokens(ob_ctx), gqa_out(oc_ctx)], axis=-1)
            ctx = ctx + m_ctx[2] * (o_ctx @ w_out[layer])

        g_lat = modulate(rms_norm(x, norm2_gain[layer]), m_lat[3], m_lat[4])
        x = x + m_lat[5] * peer_ffn(g_lat, peer_w_q[layer], peer_subkeys[layer], peer_u[layer], peer_v[layer])
        if need_ctx:
            g_ctx = modulate(rms_norm(ctx, norm2_gain[layer]), m_ctx[3], m_ctx[4])
            ctx = ctx + m_ctx[5] * peer_ffn(g_ctx, peer_w_q[layer], peer_subkeys[layer], peer_u[layer], peer_v[layer])

    return x
```

```python
import functools

import jax
import jax.numpy as jnp
from jax import lax
from jax.experimental import pallas as pl
from jax.experimental.pallas import tpu as pltpu

F32 = jnp.float32
BF16 = jnp.bfloat16

LANE = 128
HEAD_DIM = 128
QK_NOPE = 128
QK_ROPE = 64
B_QK = QK_NOPE + QK_ROPE
B_QK_PAD = 2 * LANE
GRID_W = 64
WINDOW = 128
CHUNK = 128
PEER_TOPK = 16
N_MOD = 6
ROPE_BASE = 10000.0
EPS = 1e-6
NEG = -1e30
MIB = 1024 * 1024
VMEM_LIMIT = 56 * MIB


def _cparams(semantics, vmem=VMEM_LIMIT):
    return pltpu.CompilerParams(dimension_semantics=semantics, vmem_limit_bytes=vmem)


def _pick(n, target, align=LANE):
    if n <= target:
        return n
    best = None
    d = align
    while d <= target:
        if n % d == 0:
            best = d
        d += align
    assert best is not None, (n, target, align)
    return best


def _gelu(x):
    return 0.5 * x * (1.0 + jnp.tanh(0.7978845608028654 * (x + 0.044715 * (x * x * x))))


def _dot_nt(a, b):
    return lax.dot_general(a, b, (((1,), (1,)), ((), ())), preferred_element_type=F32)


def _modulation_kernel(c_ref, w_ref, b_ref, o_ref, acc_ref):
    k = pl.program_id(1)

    @pl.when(k == 0)
    def _():
        acc_ref[...] = jnp.zeros_like(acc_ref)

    c = c_ref[...]
    silu = c / (1.0 + jnp.exp(-c))
    acc_ref[...] += jnp.dot(silu.astype(BF16), w_ref[...].astype(BF16), preferred_element_type=F32)

    @pl.when(k == pl.num_programs(1) - 1)
    def _():
        o_ref[...] = acc_ref[...] + b_ref[...]


def _modulation(c8, w, b):
    rows, d = c8.shape
    n = w.shape[1]
    bn = _pick(n, 2048)
    bk = _pick(d, 1024)
    return pl.pallas_call(
        _modulation_kernel,
        grid=(n // bn, d // bk),
        in_specs=[pl.BlockSpec((rows, bk), lambda j, k: (0, k)),
                  pl.BlockSpec((bk, bn), lambda j, k: (k, j)),
                  pl.BlockSpec((1, bn), lambda j, k: (0, j))],
        out_specs=pl.BlockSpec((rows, bn), lambda j, k: (0, j)),
        out_shape=jax.ShapeDtypeStruct((rows, n), F32),
        scratch_shapes=[pltpu.VMEM((rows, bn), F32)],
        compiler_params=_cparams(("parallel", "arbitrary")),
        name="modulation",
    )(c8, w, b.reshape(1, n))


class _Rows:
    def __init__(self, batch, seq, ctx_len):
        self.batch, self.seq, self.ctx_len = batch, seq, ctx_len
        self.m_lat = batch * seq
        self.m_ctx = batch * ctx_len
        self.m = self.m_lat + self.m_ctx

    def block(self, target):
        bm = target
        while self.seq % bm or self.m_ctx % bm:
            bm //= 2
        assert bm >= 8
        return bm

    def mod_row(self, i, bm):
        return jnp.where(i < self.m_lat // bm, (i * bm) // self.seq, self.batch)


def _norm_mod_kernel(x_ref, gain_ref, shift_ref, scale_ref, *out_refs):
    x = x_ref[...]
    y = x * lax.rsqrt(jnp.mean(x * x, axis=-1, keepdims=True) + EPS) * gain_ref[...]
    h = y * (1.0 + scale_ref[0]) + shift_ref[0]
    for o in out_refs:
        o[...] = h.astype(o.dtype)


def _norm_mod(x, gain, mod3, shift_chunk, scale_chunk, rows, out_dtypes):
    m, d = x.shape
    bm = rows.block(256)
    outs = pl.pallas_call(
        _norm_mod_kernel,
        grid=(m // bm,),
        in_specs=[pl.BlockSpec((bm, d), lambda i: (i, 0)),
                  pl.BlockSpec((1, d), lambda i: (0, 0)),
                  pl.BlockSpec((1, 1, d), lambda i: (rows.mod_row(i, bm), 0, shift_chunk)),
                  pl.BlockSpec((1, 1, d), lambda i: (rows.mod_row(i, bm), 0, scale_chunk))],
        out_specs=[pl.BlockSpec((bm, d), lambda i: (i, 0)) for _ in out_dtypes],
        out_shape=[jax.ShapeDtypeStruct((m, d), dt) for dt in out_dtypes],
        compiler_params=_cparams(("parallel",)),
        name="norm_mod",
    )(x, gain.reshape(1, d), mod3, mod3)
    return outs


def _mm_kernel(a_ref, w_ref, o_ref):
    o_ref[...] = jnp.dot(a_ref[...], w_ref[...], preferred_element_type=F32).astype(o_ref.dtype)


def _mm_res_kernel(a_ref, w_ref, res_ref, gate_ref, o_ref):
    acc = jnp.dot(a_ref[...], w_ref[...], preferred_element_type=F32)
    o_ref[...] = res_ref[...] + gate_ref[0] * acc


def _matmul(a, w, out_dtype=F32, bm_target=1024, bn_target=1024):
    m, k = a.shape
    n = w.shape[1]
    bm = _pick(m, bm_target, 8)
    bn = _pick(n, bn_target)
    return pl.pallas_call(
        _mm_kernel,
        grid=(m // bm, n // bn),
        in_specs=[pl.BlockSpec((bm, k), lambda i, j: (i, 0)),
                  pl.BlockSpec((k, bn), lambda i, j: (0, j))],
        out_specs=pl.BlockSpec((bm, bn), lambda i, j: (i, j)),
        out_shape=jax.ShapeDtypeStruct((m, n), out_dtype),
        compiler_params=_cparams(("parallel", "parallel")),
        name="matmul",
    )(a, w)


def _matmul_residual(a, w, res, mod3, gate_chunk, rows):
    m, k = a.shape
    n = w.shape[1]
    bm = rows.block(1024)
    bn = _pick(n, 512)
    return pl.pallas_call(
        _mm_res_kernel,
        grid=(m // bm, n // bn),
        in_specs=[pl.BlockSpec((bm, k), lambda i, j: (i, 0)),
                  pl.BlockSpec((k, bn), lambda i, j: (0, j)),
                  pl.BlockSpec((bm, bn), lambda i, j: (i, j)),
                  pl.BlockSpec((1, 1, bn),
                               lambda i, j: (rows.mod_row(i, bm), 0, gate_chunk * (n // bn) + j))],
        out_specs=pl.BlockSpec((bm, bn), lambda i, j: (i, j)),
        out_shape=jax.ShapeDtypeStruct((m, n), F32),
        compiler_params=_cparams(("parallel", "parallel")),
        name="matmul_residual",
    )(a, w, res, mod3)


def _gated_add_kernel(res_ref, y_ref, gate_ref, o_ref):
    o_ref[...] = res_ref[...] + gate_ref[0] * y_ref[...]


def _gated_add(res, y, mod3, gate_chunk, rows):
    m, d = res.shape
    bm = rows.block(256)
    return pl.pallas_call(
        _gated_add_kernel,
        grid=(m // bm,),
        in_specs=[pl.BlockSpec((bm, d), lambda i: (i, 0)),
                  pl.BlockSpec((bm, d), lambda i: (i, 0)),
                  pl.BlockSpec((1, 1, d), lambda i: (rows.mod_row(i, bm), 0, gate_chunk))],
        out_specs=pl.BlockSpec((bm, d), lambda i: (i, 0)),
        out_shape=jax.ShapeDtypeStruct((m, d), F32),
        compiler_params=_cparams(("parallel",)),
        name="gated_add",
    )(res, y, mod3)


def _gmlp_kernel(p_ref, gain_ref, ws_ref, bs_ref, o_ref, *, width, chunks):
    ws = ws_ref[...].astype(BF16)
    bs = bs_ref[...]
    for c in range(chunks):
        r0 = c * CHUNK
        z = _gelu(p_ref[r0:r0 + CHUNK, :])
        u = z[:, :width]
        parts = []
        for h in range(width // HEAD_DIM):
            vh = z[:, width + h * HEAD_DIM: width + (h + 1) * HEAD_DIM]
            vh = vh * lax.rsqrt(jnp.mean(vh * vh, axis=-1, keepdims=True) + EPS)
            parts.append((vh * gain_ref[:, h * HEAD_DIM:(h + 1) * HEAD_DIM]).astype(BF16))
        vn = jnp.concatenate(parts, axis=1)
        s = jnp.dot(ws, vn, preferred_element_type=F32) + bs
        o_ref[r0:r0 + CHUNK, :] = (u * s).astype(o_ref.dtype)


def _gmlp(pa, v_gain, w_s, b_s):
    m, two_w = pa.shape
    width = two_w // 2
    chunks = 2
    bm = chunks * CHUNK
    return pl.pallas_call(
        functools.partial(_gmlp_kernel, width=width, chunks=chunks),
        grid=(m // bm,),
        in_specs=[pl.BlockSpec((bm, two_w), lambda i: (i, 0)),
                  pl.BlockSpec((1, width), lambda i: (0, 0)),
                  pl.BlockSpec((CHUNK, CHUNK), lambda i: (0, 0)),
                  pl.BlockSpec((CHUNK, 1), lambda i: (0, 0))],
        out_specs=pl.BlockSpec((bm, width), lambda i: (i, 0)),
        out_shape=jax.ShapeDtypeStruct((m, width), BF16),
        compiler_params=_cparams(("parallel",)),
        name="gmlp",
    )(pa, v_gain.reshape(1, width), w_s, b_s.reshape(CHUNK, 1))


def _rope(x, cos, sin_signed, half):
    lane = lax.broadcasted_iota(jnp.int32, x.shape, 1)
    first = (lane & (2 * half - 1)) < half
    partner = jnp.where(first, pltpu.roll(x, LANE - half, axis=1), pltpu.roll(x, half, axis=1))
    return x * cos + partner * sin_signed


def _mla_pre_kernel(p_ref, qg_ref, kvg_ref, cq_ref, ckv_ref, *, q_lora, kv_lora):
    cq = p_ref[:, :q_lora]
    cq_ref[...] = (cq * lax.rsqrt(jnp.mean(cq * cq, axis=-1, keepdims=True) + EPS)
                   * qg_ref[...]).astype(cq_ref.dtype)
    ckv = p_ref[:, q_lora:q_lora + kv_lora]
    ckv_ref[...] = (ckv * lax.rsqrt(jnp.mean(ckv * ckv, axis=-1, keepdims=True) + EPS)
                    * kvg_ref[...]).astype(ckv_ref.dtype)


def _mla_pre(pb, q_gain, kv_gain, rows):
    m, n = pb.shape
    q_lora, kv_lora = q_gain.shape[0], kv_gain.shape[0]
    bm = rows.block(256)
    return pl.pallas_call(
        functools.partial(_mla_pre_kernel, q_lora=q_lora, kv_lora=kv_lora),
        grid=(m // bm,),
        in_specs=[pl.BlockSpec((bm, n), lambda i: (i, 0)),
                  pl.BlockSpec((1, q_lora), lambda i: (0, 0)),
                  pl.BlockSpec((1, kv_lora), lambda i: (0, 0))],
        out_specs=[pl.BlockSpec((bm, q_lora), lambda i: (i, 0)),
                   pl.BlockSpec((bm, kv_lora), lambda i: (i, 0))],
        out_shape=[jax.ShapeDtypeStruct((m, q_lora), BF16),
                   jax.ShapeDtypeStruct((m, kv_lora), BF16)],
        compiler_params=_cparams(("parallel",)),
        name="mla_pre",
    )(pb, q_gain.reshape(1, q_lora), kv_gain.reshape(1, kv_lora))


def _mla_post_kernel(q_ref, kv_ref, kr_ref, qgn_ref, qgr_ref, kgn_ref, kgr_ref, cos_ref, sin_ref,
                     qh_ref, kh_ref, vh_ref, *, heads):
    cos, sin = cos_ref[...], sin_ref[...]
    lane = lax.broadcasted_iota(jnp.int32, cos.shape, 1)
    low = lane < QK_ROPE
    half = QK_ROPE // 4
    zeros = jnp.zeros(cos.shape, F32)

    kr = jnp.where(low, kr_ref[...], 0.0)
    kr_ss = jnp.sum(kr * kr, axis=-1, keepdims=True)
    kr_rot = _rope(kr * kgr_ref[...], cos, sin, half)

    for hp in range(heads // 2):
        qr = q_ref[:, heads * QK_NOPE + hp * LANE: heads * QK_NOPE + (hp + 1) * LANE]
        qr2 = qr * qr
        ss_lo = jnp.sum(jnp.where(low, qr2, 0.0), axis=-1, keepdims=True)
        ss_hi = jnp.sum(jnp.where(low, 0.0, qr2), axis=-1, keepdims=True)
        rinv = []
        for j, ss_r in enumerate((ss_lo, ss_hi)):
            h = 2 * hp + j
            qn = q_ref[:, h * QK_NOPE:(h + 1) * QK_NOPE]
            r = lax.rsqrt((jnp.sum(qn * qn, axis=-1, keepdims=True) + ss_r) * (1.0 / B_QK) + EPS)
            rinv.append(r)
            qh_ref[h, :, :QK_NOPE] = (qn * r * qgn_ref[...]).astype(qh_ref.dtype)
        qrot = _rope(qr * jnp.where(low, rinv[0], rinv[1]) * qgr_ref[...], cos, sin, half)
        qh_ref[2 * hp, :, QK_NOPE:] = jnp.where(low, qrot, zeros).astype(qh_ref.dtype)
        qh_ref[2 * hp + 1, :, QK_NOPE:] = jnp.where(
            low, pltpu.roll(qrot, QK_ROPE, axis=1), zeros).astype(qh_ref.dtype)

    for h in range(heads):
        kn = kv_ref[:, h * QK_NOPE:(h + 1) * QK_NOPE]
        r = lax.rsqrt((jnp.sum(kn * kn, axis=-1, keepdims=True) + kr_ss) * (1.0 / B_QK) + EPS)
        kh_ref[h, :, :QK_NOPE] = (kn * r * kgn_ref[...]).astype(kh_ref.dtype)
        kh_ref[h, :, QK_NOPE:] = (kr_rot * r).astype(kh_ref.dtype)
        vh_ref[h] = kv_ref[:, (heads + h) * HEAD_DIM:(heads + h + 1) * HEAD_DIM].astype(vh_ref.dtype)


def _mla_post(q_raw, kv_raw, pb, kr_block, qn_gain, kn_gain, cos_t, sin_t, rows, heads):
    m = q_raw.shape[0]
    bm = rows.block(256)
    n_lat, n_tab = rows.m_lat // bm, rows.seq // bm
    tab = lambda i: (jnp.where(i < n_lat, i % n_tab, n_tab), 0)
    pair = lambda g: jnp.concatenate([g, g]).reshape(1, LANE)
    return pl.pallas_call(
        functools.partial(_mla_post_kernel, heads=heads),
        grid=(m // bm,),
        in_specs=[pl.BlockSpec((bm, q_raw.shape[1]), lambda i: (i, 0)),
                  pl.BlockSpec((bm, kv_raw.shape[1]), lambda i: (i, 0)),
                  pl.BlockSpec((bm, LANE), lambda i: (i, kr_block)),
                  pl.BlockSpec((1, QK_NOPE), lambda i: (0, 0)),
                  pl.BlockSpec((1, LANE), lambda i: (0, 0)),
                  pl.BlockSpec((1, QK_NOPE), lambda i: (0, 0)),
                  pl.BlockSpec((1, LANE), lambda i: (0, 0)),
                  pl.BlockSpec((bm, LANE), tab),
                  pl.BlockSpec((bm, LANE), tab)],
        out_specs=[pl.BlockSpec((heads, bm, B_QK_PAD), lambda i: (0, i, 0)),
                   pl.BlockSpec((heads, bm, B_QK_PAD), lambda i: (0, i, 0)),
                   pl.BlockSpec((heads, bm, HEAD_DIM), lambda i: (0, i, 0))],
        out_shape=[jax.ShapeDtypeStruct((heads, m, B_QK_PAD), BF16),
                   jax.ShapeDtypeStruct((heads, m, B_QK_PAD), BF16),
                   jax.ShapeDtypeStruct((heads, m, HEAD_DIM), BF16)],
        compiler_params=_cparams(("parallel",)),
        name="mla_post",
    )(q_raw, kv_raw, pb, qn_gain[:QK_NOPE].reshape(1, QK_NOPE), pair(qn_gain[QK_NOPE:]),
      kn_gain[:QK_NOPE].reshape(1, QK_NOPE), pair(kn_gain[QK_NOPE:]), cos_t, sin_t)


def _mla_attn_kernel(q_ref, kc_ref, vc_ref, *rest, with_lat):
    if with_lat:
        kl_ref, vl_ref, o_ref = rest
    else:
        (o_ref,) = rest
    scale = B_QK ** -0.5
    q = q_ref[...]
    s_c = _dot_nt(q, kc_ref[...]) * scale
    m = jnp.max(s_c, axis=-1, keepdims=True)
    if with_lat:
        s_l = _dot_nt(q, kl_ref[...]) * scale
        m = jnp.maximum(m, jnp.max(s_l, axis=-1, keepdims=True))
    p_c = jnp.exp(s_c - m)
    den = jnp.sum(p_c, axis=-1, keepdims=True)
    acc = jnp.dot(p_c.astype(BF16), vc_ref[...], preferred_element_type=F32)
    if with_lat:
        p_l = jnp.exp(s_l - m)
        den = den + jnp.sum(p_l, axis=-1, keepdims=True)
        acc = acc + jnp.dot(p_l.astype(BF16), vl_ref[...], preferred_element_type=F32)
    o_ref[...] = (acc / den).astype(o_ref.dtype)


def _mla_attention(qh, kh, vh, rows, heads, latent_queries):
    b, s, c = rows.batch, rows.seq, rows.ctx_len
    ctx_blk0 = rows.m_lat // c
    if latent_queries:
        tq = _pick(s, 512, 8)
        nq = s // tq
        q_map = lambda bi, h, i: (h, bi * nq + i, 0)
        o_map = lambda bi, h, i: (bi * nq + i, h)
        m_out = rows.m_lat
    else:
        tq, nq = c, 1
        q_map = lambda bi, h, i: (h, ctx_blk0 + bi, 0)
        o_map = lambda bi, h, i: (bi, h)
        m_out = rows.m_ctx
    in_specs = [pl.BlockSpec((None, tq, B_QK_PAD), q_map),
                pl.BlockSpec((None, c, B_QK_PAD), lambda bi, h, i: (h, ctx_blk0 + bi, 0)),
                pl.BlockSpec((None, c, HEAD_DIM), lambda bi, h, i: (h, ctx_blk0 + bi, 0))]
    args = [qh, kh, vh]
    if latent_queries:
        in_specs += [pl.BlockSpec((None, s, B_QK_PAD), lambda bi, h, i: (h, bi, 0)),
                     pl.BlockSpec((None, s, HEAD_DIM), lambda bi, h, i: (h, bi, 0))]
        args += [kh, vh]
    return pl.pallas_call(
        functools.partial(_mla_attn_kernel, with_lat=latent_queries),
        grid=(b, heads, nq),
        in_specs=in_specs,
        out_specs=pl.BlockSpec((tq, HEAD_DIM), o_map),
        out_shape=jax.ShapeDtypeStruct((m_out, heads * HEAD_DIM), BF16),
        compiler_params=_cparams(("parallel", "parallel", "parallel")),
        name="mla_attention",
    )(*args)


def _gqa_post_kernel(p_ref, qg_ref, kg_ref, cos_ref, sin_ref, q_ref, k_ref, v_ref, *, q_heads, kv_heads):
    cos, sin = cos_ref[...], sin_ref[...]
    half = HEAD_DIM // 4

    def norm_rope(x, gain):
        xn = x * lax.rsqrt(jnp.mean(x * x, axis=-1, keepdims=True) + EPS) * gain
        return _rope(xn, cos, sin, half)

    for h in range(q_heads):
        sl = slice(h * HEAD_DIM, (h + 1) * HEAD_DIM)
        q_ref[:, sl] = norm_rope(p_ref[:, sl], qg_ref[...]).astype(q_ref.dtype)
    for h in range(kv_heads):
        sl = slice(h * HEAD_DIM, (h + 1) * HEAD_DIM)
        k0 = q_heads * HEAD_DIM
        v0 = (q_heads + kv_heads) * HEAD_DIM
        k_ref[:, sl] = norm_rope(p_ref[:, k0 + h * HEAD_DIM:k0 + (h + 1) * HEAD_DIM],
                                 kg_ref[...]).astype(k_ref.dtype)
        v_ref[:, sl] = p_ref[:, v0 + h * HEAD_DIM:v0 + (h + 1) * HEAD_DIM].astype(v_ref.dtype)


def _gqa_post(pc, qn_gain, kn_gain, cos_t, sin_t, rows, q_heads, kv_heads):
    m, n = pc.shape
    bm = rows.block(256)
    n_lat, n_tab = rows.m_lat // bm, rows.seq // bm
    tab = lambda i: (jnp.where(i < n_lat, i % n_tab, n_tab), 0)
    return pl.pallas_call(
        functools.partial(_gqa_post_kernel, q_heads=q_heads, kv_heads=kv_heads),
        grid=(m // bm,),
        in_specs=[pl.BlockSpec((bm, n), lambda i: (i, 0)),
                  pl.BlockSpec((1, HEAD_DIM), lambda i: (0, 0)),
                  pl.BlockSpec((1, HEAD_DIM), lambda i: (0, 0)),
                  pl.BlockSpec((bm, LANE), tab),
                  pl.BlockSpec((bm, LANE), tab)],
        out_specs=[pl.BlockSpec((bm, q_heads * HEAD_DIM), lambda i: (i, 0)),
                   pl.BlockSpec((bm, kv_heads * HEAD_DIM), lambda i: (i, 0)),
                   pl.BlockSpec((bm, kv_heads * HEAD_DIM), lambda i: (i, 0))],
        out_shape=[jax.ShapeDtypeStruct((m, q_heads * HEAD_DIM), BF16),
                   jax.ShapeDtypeStruct((m, kv_heads * HEAD_DIM), BF16),
                   jax.ShapeDtypeStruct((m, kv_heads * HEAD_DIM), BF16)],
        compiler_params=_cparams(("parallel",)),
        name="gqa_post",
    )(pc, qn_gain.reshape(1, HEAD_DIM), kn_gain.reshape(1, HEAD_DIM), cos_t, sin_t)


def _gqa_attn_kernel(sink_ref, q_ref, kc_ref, vc_ref, *rest, group, seq, banded):
    if banded:
        kp_ref, kq_ref, kn_ref, vp_ref, vq_ref, vn_ref, o_ref = rest
    else:
        (o_ref,) = rest
    scale = HEAD_DIM ** -0.5
    kvh = pl.program_id(1)
    blk = pl.program_id(2)
    rows_q = group * WINDOW
    q = jnp.concatenate([q_ref[:, g * HEAD_DIM:(g + 1) * HEAD_DIM] for g in range(group)], axis=0)

    row = lax.broadcasted_iota(jnp.int32, (rows_q, 1), 0)
    sink = jnp.zeros((rows_q, 1), F32)
    for g in range(group):
        in_g = (row >= g * WINDOW) & (row < (g + 1) * WINDOW)
        sink = jnp.where(in_g, sink_ref[kvh * group + g], sink)

    s_c = _dot_nt(q, kc_ref[...]) * scale
    m = jnp.maximum(jnp.max(s_c, axis=-1, keepdims=True), sink)
    if banded:
        kb = jnp.concatenate([kp_ref[...], kq_ref[...], kn_ref[...]], axis=0)
        vb = jnp.concatenate([vp_ref[...], vq_ref[...], vn_ref[...]], axis=0)
        s_b = _dot_nt(q, kb) * scale
        qpos = lax.broadcasted_iota(jnp.int32, s_b.shape, 0) & (WINDOW - 1)
        krel = lax.broadcasted_iota(jnp.int32, s_b.shape, 1) - WINDOW
        kpos = blk * WINDOW + krel
        valid = (jnp.abs(qpos - krel) <= WINDOW) & (kpos >= 0) & (kpos < seq)
        s_b = jnp.where(valid, s_b, NEG)
        m = jnp.maximum(m, jnp.max(s_b, axis=-1, keepdims=True))
    p_c = jnp.exp(s_c - m)
    den = jnp.sum(p_c, axis=-1, keepdims=True) + jnp.exp(sink - m)
    acc = jnp.dot(p_c.astype(BF16), vc_ref[...], preferred_element_type=F32)
    if banded:
        p_b = jnp.exp(s_b - m)
        den = den + jnp.sum(p_b, axis=-1, keepdims=True)
        acc = acc + jnp.dot(p_b.astype(BF16), vb, preferred_element_type=F32)
    o = acc / den
    for g in range(group):
        o_ref[:, g * HEAD_DIM:(g + 1) * HEAD_DIM] = o[g * WINDOW:(g + 1) * WINDOW].astype(o_ref.dtype)


def _gqa_attention(qc, kc, vc, sink, rows, q_heads, kv_heads, latent_queries):
    b, s, c = rows.batch, rows.seq, rows.ctx_len
    group = q_heads // kv_heads
    gw = group * HEAD_DIM
    ctx_blk0 = rows.m_lat // c
    ctx_spec = lambda: pl.BlockSpec((c, HEAD_DIM), lambda bi, k, n: (ctx_blk0 + bi, k))
    if latent_queries:
        nb = s // WINDOW
        qrow = lambda bi, n: bi * nb + n
        m_out = rows.m_lat
        orow = qrow
    else:
        nb = c // WINDOW
        qrow = lambda bi, n: rows.m_lat // WINDOW + bi * nb + n
        orow = lambda bi, n: bi * nb + n
        m_out = rows.m_ctx
    in_specs = [pl.BlockSpec(memory_space=pltpu.SMEM),
                pl.BlockSpec((WINDOW, gw), lambda bi, k, n: (qrow(bi, n), k)),
                ctx_spec(), ctx_spec()]
    args = [sink, qc, kc, vc]
    if latent_queries:
        band = [lambda bi, k, n: (bi * nb + jnp.maximum(n - 1, 0), k),
                lambda bi, k, n: (bi * nb + n, k),
                lambda bi, k, n: (bi * nb + jnp.minimum(n + 1, nb - 1), k)]
        in_specs += [pl.BlockSpec((WINDOW, HEAD_DIM), f) for f in band] * 2
        args += [kc, kc, kc, vc, vc, vc]
    return pl.pallas_call(
        functools.partial(_gqa_attn_kernel, group=group, seq=s, banded=latent_queries),
        grid=(b, kv_heads, nb),
        in_specs=in_specs,
        out_specs=pl.BlockSpec((WINDOW, gw), lambda bi, k, n: (orow(bi, n), k)),
        out_shape=jax.ShapeDtypeStruct((m_out, q_heads * HEAD_DIM), BF16),
        compiler_params=_cparams(("parallel", "parallel", "parallel")),
        name="gqa_attention",
    )(*args)


def _extract_topk(s, ids, count, val_ref, id_ref):
    n = s.shape[0]
    pos = lax.broadcasted_iota(jnp.int32, s.shape, 0).astype(F32)
    for r in range(count):
        top = jnp.max(s, axis=0, keepdims=True)
        first = jnp.min(jnp.where(s == top, pos, float(n)), axis=0, keepdims=True)
        hit = pos == first
        val_ref[r:r + 1, :] = top
        id_ref[r:r + 1, :] = jnp.max(jnp.where(hit, ids, -1.0), axis=0, keepdims=True)
        s = jnp.where(hit, -jnp.inf, s)


def _route_kernel(q_ref, sk_ref, idx_ref, gate_ref, tv_ref, ti_ref, cs_ref, ci_ref, bs_ref, bi_ref,
                  *, n_keys):
    key_id = lax.broadcasted_iota(jnp.int32, (n_keys, q_ref.shape[0]), 0).astype(F32)
    for p in range(2):
        qp = q_ref[:, p * LANE:(p + 1) * LANE].astype(BF16)
        scores = _dot_nt(sk_ref[p], qp)
        _extract_topk(scores, key_id, PEER_TOPK, tv_ref.at[p], ti_ref.at[p])
    for a in range(PEER_TOPK):
        rs = slice(a * PEER_TOPK, (a + 1) * PEER_TOPK)
        cs_ref[rs, :] = tv_ref[0, a:a + 1, :] + tv_ref[1]
        ci_ref[rs, :] = ti_ref[0, a:a + 1, :] * float(n_keys) + ti_ref[1]
    _extract_topk(cs_ref[...], ci_ref[...], PEER_TOPK, bs_ref, bi_ref)
    best = bs_ref[...]
    e = jnp.exp(best - best[0:1, :])
    gate_ref[...] = e / jnp.sum(e, axis=0, keepdims=True)
    idx_ref[...] = bi_ref[...].astype(jnp.int32)


def _peer_route(pq, subkeys):
    m = pq.shape[0]
    heads, _, n_keys, half = subkeys.shape
    assert half == LANE
    tt = _pick(m, 256)
    k = PEER_TOPK
    return pl.pallas_call(
        functools.partial(_route_kernel, n_keys=n_keys),
        grid=(m // tt, heads),
        in_specs=[pl.BlockSpec((tt, 2 * LANE), lambda i, h: (i, h)),
                  pl.BlockSpec((None, 2, n_keys, half), lambda i, h: (h, 0, 0, 0))],
        out_specs=[pl.BlockSpec((None, k, tt), lambda i, h: (h, 0, i)),
                   pl.BlockSpec((None, k, tt), lambda i, h: (h, 0, i))],
        out_shape=[jax.ShapeDtypeStruct((heads, k, m), jnp.int32),
                   jax.ShapeDtypeStruct((heads, k, m), F32)],
        scratch_shapes=[pltpu.VMEM((2, k, tt), F32), pltpu.VMEM((2, k, tt), F32),
                        pltpu.VMEM((k * k, tt), F32), pltpu.VMEM((k * k, tt), F32),
                        pltpu.VMEM((k, tt), F32), pltpu.VMEM((k, tt), F32)],
        compiler_params=_cparams(("parallel", "parallel")),
        name="peer_route",
    )(pq, subkeys.astype(BF16))


def _peer_kernel(idx_ref, g_ref, gate_ref, u_hbm, v_hbm, o_ref, ubuf, vbuf, sem, *, tokens, n_sel):
    def rows_copy(tab, buf, slot, which, e, k):
        return pltpu.make_async_copy(tab.at[pl.ds(e, 1), :], buf.at[slot, pl.ds(k, 1), :],
                                     sem.at[which, slot])

    def issue(t, slot):
        for k in range(n_sel):
            e = idx_ref[t, k]
            rows_copy(u_hbm, ubuf, slot, 0, e, k).start()
            rows_copy(v_hbm, vbuf, slot, 1, e, k).start()

    def wait(tab, buf, slot, which):
        pltpu.make_async_copy(tab.at[pl.ds(0, n_sel), :], buf.at[slot], sem.at[which, slot]).wait()

    eye = (lax.broadcasted_iota(jnp.int32, (n_sel, n_sel), 0)
           == lax.broadcasted_iota(jnp.int32, (n_sel, n_sel), 1))

    def compute(t, slot):
        g = g_ref[pl.ds(t, 1), :]
        wait(u_hbm, ubuf, slot, 0)
        d = jnp.sum(ubuf[slot] * g, axis=1, keepdims=True)
        gate_row = gate_ref[pl.ds(t, 1), :]
        gate_col = jnp.sum(jnp.where(eye, gate_row, 0.0), axis=1, keepdims=True)
        w = gate_col * _gelu(d)
        wait(v_hbm, vbuf, slot, 1)
        o_ref[pl.ds(t, 1), :] = jnp.sum(vbuf[slot] * w, axis=0, keepdims=True)

    issue(0, 0)

    def pair(i, carry):
        t = 2 * i
        issue(t + 1, 1)
        compute(t, 0)

        @pl.when(t + 2 < tokens)
        def _():
            issue(t + 2, 0)

        compute(t + 1, 1)
        return carry

    lax.fori_loop(0, tokens // 2, pair, 0)


def _peer_experts(idx, gate, g, u_tab, v_tab):
    m, d = g.shape
    n_sel = idx.shape[1]
    tb = _pick(m, 64, 8)
    assert tb % 2 == 0
    return pl.pallas_call(
        functools.partial(_peer_kernel, tokens=tb, n_sel=n_sel),
        grid=(m // tb,),
        in_specs=[pl.BlockSpec((tb, n_sel), lambda i: (i, 0), memory_space=pltpu.SMEM),
                  pl.BlockSpec((tb, d), lambda i: (i, 0)),
                  pl.BlockSpec((tb, n_sel), lambda i: (i, 0)),
                  pl.BlockSpec(memory_space=pl.ANY),
                  pl.BlockSpec(memory_space=pl.ANY)],
        out_specs=pl.BlockSpec((tb, d), lambda i: (i, 0)),
        out_shape=jax.ShapeDtypeStruct((m, d), F32),
        scratch_shapes=[pltpu.VMEM((2, n_sel, d), F32), pltpu.VMEM((2, n_sel, d), F32),
                        pltpu.SemaphoreType.DMA((2, 2))],
        compiler_params=_cparams(("arbitrary",)),
        name="peer_experts",
    )(idx, g, gate, u_tab, v_tab)


def _rope_tables(seq, rot_dim, pad):
    rows = seq // GRID_W
    row = jnp.repeat(jnp.arange(rows, dtype=F32), GRID_W)
    col = jnp.tile(jnp.arange(GRID_W, dtype=F32), rows)
    n_freq = rot_dim // 4
    inv = ROPE_BASE ** (-jnp.arange(n_freq, dtype=F32) / n_freq)
    ar, ac = row[:, None] * inv, col[:, None] * inv
    ang = jnp.concatenate([ar, ar, ac, ac], axis=-1)
    sign = jnp.concatenate([-jnp.ones(n_freq), jnp.ones(n_freq)] * 2).astype(F32)
    cos, sin = jnp.cos(ang), jnp.sin(ang) * sign
    reps = LANE // rot_dim
    cos, sin = jnp.tile(cos, (1, reps)), jnp.tile(sin, (1, reps))
    cos = jnp.concatenate([cos, jnp.ones((pad, LANE), F32)], axis=0)
    sin = jnp.concatenate([sin, jnp.zeros((pad, LANE), F32)], axis=0)
    return cos, sin


def kernel(x, c, ctx, c_ctx, w_mod, b_mod, norm1_gain, norm2_gain, w_in, a_v_gain, a_w_s, a_b_s,
           b_q_gain, b_kv_gain, b_w_uq, b_w_ukv, b_qn_gain, b_kn_gain, c_qn_gain, c_kn_gain, c_sink,
           w_out, peer_w_q, peer_subkeys, peer_u, peer_v):
    batch, seq, d = x.shape
    ctx_len = ctx.shape[1]
    depth = w_mod.shape[0]
    a_width = a_v_gain.shape[1]
    q_lora, kv_lora = b_q_gain.shape[1], b_kv_gain.shape[1]
    b_heads = b_w_uq.shape[2] // B_QK
    c_heads = c_sink.shape[1]
    c_kv = c_heads // 3
    p_a = 2 * a_width
    p_b = q_lora + kv_lora + QK_ROPE
    p_b_pad = -(-p_b // (2 * LANE)) * (2 * LANE)
    assert b_heads % 2 == 0 and (q_lora + kv_lora) % LANE == 0
    assert seq % WINDOW == 0 and ctx_len % WINDOW == 0

    rows_all = _Rows(batch, seq, ctx_len)
    bm_tab = rows_all.block(256)
    cos_b, sin_b = _rope_tables(seq, QK_ROPE, bm_tab)
    cos_c, sin_c = _rope_tables(seq, HEAD_DIM, bm_tab)

    c8 = jnp.zeros((8, d), F32).at[:batch].set(c).at[batch].set(c_ctx)
    xs = jnp.concatenate([x.reshape(batch * seq, d), ctx.reshape(batch * ctx_len, d)], axis=0)

    for layer in range(depth):
        need_ctx = layer < depth - 1
        mod3 = _modulation(c8, w_mod[layer], b_mod[layer]).reshape(8, 1, N_MOD * d)

        w_l = w_in[layer]
        w_a = w_l[:, :p_a].astype(BF16)
        w_b = jnp.pad(w_l[:, p_a:p_a + p_b], ((0, 0), (0, p_b_pad - p_b))).astype(BF16)
        w_c = w_l[:, p_a + p_b:].astype(BF16)
        w_uq = b_w_uq[layer].reshape(q_lora, b_heads, B_QK)
        w_uq = jnp.concatenate([w_uq[:, :, :QK_NOPE].reshape(q_lora, -1),
                                w_uq[:, :, QK_NOPE:].reshape(q_lora, -1)], axis=1).astype(BF16)
        w_ukv = b_w_ukv[layer].reshape(kv_lora, b_heads, QK_NOPE + HEAD_DIM)
        w_ukv = jnp.concatenate([w_ukv[:, :, :QK_NOPE].reshape(kv_lora, -1),
                                 w_ukv[:, :, QK_NOPE:].reshape(kv_lora, -1)], axis=1).astype(BF16)

        (h,) = _norm_mod(xs, norm1_gain[layer], mod3, 0, 1, rows_all, [BF16])
        pa = _matmul(h, w_a)
        pb = _matmul(h, w_b)
        pc = _matmul(h, w_c)

        oa = _gmlp(pa, a_v_gain[layer], a_w_s[layer], a_b_s[layer])

        cq, ckv = _mla_pre(pb, b_q_gain[layer], b_kv_gain[layer], rows_all)
        q_raw = _matmul(cq, w_uq)
        kv_raw = _matmul(ckv, w_ukv)
        qh, kh, vh = _mla_post(q_raw, kv_raw, pb, (q_lora + kv_lora) // LANE, b_qn_gain[layer],
                               b_kn_gain[layer], cos_b, sin_b, rows_all, b_heads)
        ob = _mla_attention(qh, kh, vh, rows_all, b_heads, True)

        qc, kc, vc = _gqa_post(pc, c_qn_gain[layer], c_kn_gain[layer], cos_c, sin_c, rows_all,
                               c_heads, c_kv)
        oc = _gqa_attention(qc, kc, vc, c_sink[layer], rows_all, c_heads, c_kv, True)

        if need_ctx:
            rows = rows_all
            ob = jnp.concatenate([ob, _mla_attention(qh, kh, vh, rows_all, b_heads, False)], axis=0)
            oc = jnp.concatenate(
                [oc, _gqa_attention(qc, kc, vc, c_sink[layer], rows_all, c_heads, c_kv, False)], axis=0)
        else:
            rows = _Rows(batch, seq, 0)
            oa = oa[:rows.m_lat]
            xs = xs[:rows.m_lat]
        o = jnp.concatenate([oa, ob, oc], axis=1)
        xs = _matmul_residual(o, w_out[layer].astype(BF16), xs, mod3, 2, rows)

        g_bf, g = _norm_mod(xs, norm2_gain[layer], mod3, 3, 4, rows, [BF16, F32])
        pq = _matmul(g_bf, peer_w_q[layer].astype(BF16))
        idx, gate = _peer_route(pq, peer_subkeys[layer])
        n_sel = idx.shape[0] * idx.shape[1]
        idx = idx.reshape(n_sel, -1).T
        gate = gate.reshape(n_sel, -1).T
        y = _peer_experts(idx, gate, g, peer_u[layer], peer_v[layer])
        xs = _gated_add(xs, y, mod3, 5, rows)

    return xs[:batch * seq].reshape(batch, seq, d)
```

```python
import functools

import jax
import jax.numpy as jnp
from jax import lax
from jax.experimental import pallas as pl
from jax.experimental.pallas import tpu as pltpu

F32 = jnp.float32
BF16 = jnp.bfloat16

LANE = 128
HEAD_DIM = 128
QK_NOPE = 128
QK_ROPE = 64
B_QK = QK_NOPE + QK_ROPE
B_QK_PAD = 2 * LANE
GRID_W = 64
WINDOW = 128
CHUNK = 128
PEER_TOPK = 16
N_MOD = 6
ROPE_BASE = 10000.0
EPS = 1e-6
NEG = -1e30
MIB = 1024 * 1024
VMEM_LIMIT = 56 * MIB


def _cparams(semantics, vmem=VMEM_LIMIT):
    return pltpu.CompilerParams(dimension_semantics=semantics, vmem_limit_bytes=vmem)


def _pick(n, target, align=LANE):
    if n <= target:
        return n
    best = None
    d = align
    while d <= target:
        if n % d == 0:
            best = d
        d += align
    assert best is not None, (n, target, align)
    return best


def _gelu(x):
    return 0.5 * x * (1.0 + jnp.tanh(0.7978845608028654 * (x + 0.044715 * (x * x * x))))


def _dot_nt(a, b):
    return lax.dot_general(a, b, (((1,), (1,)), ((), ())), preferred_element_type=F32)


def _modulation_kernel(c_ref, w_ref, b_ref, o_ref, acc_ref):
    k = pl.program_id(1)

    @pl.when(k == 0)
    def _():
        acc_ref[...] = jnp.zeros_like(acc_ref)

    c = c_ref[...]
    silu = c / (1.0 + jnp.exp(-c))
    acc_ref[...] += jnp.dot(silu.astype(BF16), w_ref[...].astype(BF16), preferred_element_type=F32)

    @pl.when(k == pl.num_programs(1) - 1)
    def _():
        o_ref[...] = acc_ref[...] + b_ref[...]


def _modulation(c8, w_all, b, layer):
    rows, d = c8.shape
    n = w_all.shape[2]
    bn = _pick(n, 2048)
    bk = _pick(d, 1024)
    return pl.pallas_call(
        _modulation_kernel,
        grid=(n // bn, d // bk),
        in_specs=[pl.BlockSpec((rows, bk), lambda j, k: (0, k)),
                  pl.BlockSpec((None, bk, bn), lambda j, k: (layer, k, j)),
                  pl.BlockSpec((1, bn), lambda j, k: (0, j))],
        out_specs=pl.BlockSpec((rows, bn), lambda j, k: (0, j)),
        out_shape=jax.ShapeDtypeStruct((rows, n), F32),
        scratch_shapes=[pltpu.VMEM((rows, bn), F32)],
        compiler_params=_cparams(("parallel", "arbitrary")),
        name="modulation",
    )(c8, w_all, b.reshape(1, n))


class _Rows:
    def __init__(self, batch, seq, ctx_len):
        self.batch, self.seq, self.ctx_len = batch, seq, ctx_len
        self.m_lat = batch * seq
        self.m_ctx = batch * ctx_len
        self.m = self.m_lat + self.m_ctx

    def block(self, target):
        bm = target
        while self.seq % bm or self.m_ctx % bm:
            bm //= 2
        assert bm >= 8
        return bm

    def mod_row(self, i, bm):
        return jnp.where(i < self.m_lat // bm, (i * bm) // self.seq, self.batch)


def _norm_mod_kernel(x_ref, gain_ref, shift_ref, scale_ref, *out_refs):
    x = x_ref[...]
    y = x * lax.rsqrt(jnp.mean(x * x, axis=-1, keepdims=True) + EPS) * gain_ref[...]
    h = y * (1.0 + scale_ref[0]) + shift_ref[0]
    for o in out_refs:
        o[...] = h.astype(o.dtype)


def _norm_mod(x, gain, mod3, shift_chunk, scale_chunk, rows, out_dtypes):
    m, d = x.shape
    bm = rows.block(256)
    outs = pl.pallas_call(
        _norm_mod_kernel,
        grid=(m // bm,),
        in_specs=[pl.BlockSpec((bm, d), lambda i: (i, 0)),
                  pl.BlockSpec((1, d), lambda i: (0, 0)),
                  pl.BlockSpec((1, 1, d), lambda i: (rows.mod_row(i, bm), 0, shift_chunk)),
                  pl.BlockSpec((1, 1, d), lambda i: (rows.mod_row(i, bm), 0, scale_chunk))],
        out_specs=[pl.BlockSpec((bm, d), lambda i: (i, 0)) for _ in out_dtypes],
        out_shape=[jax.ShapeDtypeStruct((m, d), dt) for dt in out_dtypes],
        compiler_params=_cparams(("parallel",)),
        name="norm_mod",
    )(x, gain.reshape(1, d), mod3, mod3)
    return outs


def _mm_kernel(a_ref, w_ref, o_ref):
    o_ref[...] = jnp.dot(a_ref[...], w_ref[...], preferred_element_type=F32).astype(o_ref.dtype)


def _mm_res_kernel(a_ref, w_ref, res_ref, gate_ref, o_ref):
    acc = jnp.dot(a_ref[...], w_ref[...], preferred_element_type=F32)
    o_ref[...] = res_ref[...] + gate_ref[0] * acc


def _matmul(a, w, out_dtype=F32, bm_target=1024, bn_target=1024):
    m, k = a.shape
    n = w.shape[1]
    bm = _pick(m, bm_target, 8)
    bn = _pick(n, bn_target)
    return pl.pallas_call(
        _mm_kernel,
        grid=(m // bm, n // bn),
        in_specs=[pl.BlockSpec((bm, k), lambda i, j: (i, 0)),
                  pl.BlockSpec((k, bn), lambda i, j: (0, j))],
        out_specs=pl.BlockSpec((bm, bn), lambda i, j: (i, j)),
        out_shape=jax.ShapeDtypeStruct((m, n), out_dtype),
        compiler_params=_cparams(("parallel", "parallel")),
        name="matmul",
    )(a, w)


def _matmul_residual(a, w, res, mod3, gate_chunk, rows):
    m, k = a.shape
    n = w.shape[1]
    bm = rows.block(1024)
    bn = _pick(n, 512)
    return pl.pallas_call(
        _mm_res_kernel,
        grid=(m // bm, n // bn),
        in_specs=[pl.BlockSpec((bm, k), lambda i, j: (i, 0)),
                  pl.BlockSpec((k, bn), lambda i, j: (0, j)),
                  pl.BlockSpec((bm, bn), lambda i, j: (i, j)),
                  pl.BlockSpec((1, 1, bn),
                               lambda i, j: (rows.mod_row(i, bm), 0, gate_chunk * (n // bn) + j))],
        out_specs=pl.BlockSpec((bm, bn), lambda i, j: (i, j)),
        out_shape=jax.ShapeDtypeStruct((m, n), F32),
        compiler_params=_cparams(("parallel", "parallel")),
        name="matmul_residual",
    )(a, w, res, mod3)


def _gated_add_kernel(res_ref, y_ref, gate_ref, o_ref):
    o_ref[...] = res_ref[...] + gate_ref[0] * y_ref[...]


def _gated_add(res, y, mod3, gate_chunk, rows):
    m, d = res.shape
    bm = rows.block(256)
    return pl.pallas_call(
        _gated_add_kernel,
        grid=(m // bm,),
        in_specs=[pl.BlockSpec((bm, d), lambda i: (i, 0)),
                  pl.BlockSpec((bm, d), lambda i: (i, 0)),
                  pl.BlockSpec((1, 1, d), lambda i: (rows.mod_row(i, bm), 0, gate_chunk))],
        out_specs=pl.BlockSpec((bm, d), lambda i: (i, 0)),
        out_shape=jax.ShapeDtypeStruct((m, d), F32),
        compiler_params=_cparams(("parallel",)),
        name="gated_add",
    )(res, y, mod3)


def _gmlp_kernel(p_ref, gain_ref, ws_ref, bs_ref, o_ref, *, width, chunks):
    ws = ws_ref[...].astype(BF16)
    bs = bs_ref[...]
    for c in range(chunks):
        r0 = c * CHUNK
        z = _gelu(p_ref[r0:r0 + CHUNK, :])
        u = z[:, :width]
        parts = []
        for h in range(width // HEAD_DIM):
            vh = z[:, width + h * HEAD_DIM: width + (h + 1) * HEAD_DIM]
            vh = vh * lax.rsqrt(jnp.mean(vh * vh, axis=-1, keepdims=True) + EPS)
            parts.append((vh * gain_ref[:, h * HEAD_DIM:(h + 1) * HEAD_DIM]).astype(BF16))
        vn = jnp.concatenate(parts, axis=1)
        s = jnp.dot(ws, vn, preferred_element_type=F32) + bs
        o_ref[r0:r0 + CHUNK, :] = (u * s).astype(o_ref.dtype)


def _gmlp(pa, v_gain, w_s, b_s):
    m, two_w = pa.shape
    width = two_w // 2
    chunks = 2
    bm = chunks * CHUNK
    return pl.pallas_call(
        functools.partial(_gmlp_kernel, width=width, chunks=chunks),
        grid=(m // bm,),
        in_specs=[pl.BlockSpec((bm, two_w), lambda i: (i, 0)),
                  pl.BlockSpec((1, width), lambda i: (0, 0)),
                  pl.BlockSpec((CHUNK, CHUNK), lambda i: (0, 0)),
                  pl.BlockSpec((CHUNK, 1), lambda i: (0, 0))],
        out_specs=pl.BlockSpec((bm, width), lambda i: (i, 0)),
        out_shape=jax.ShapeDtypeStruct((m, width), BF16),
        compiler_params=_cparams(("parallel",)),
        name="gmlp",
    )(pa, v_gain.reshape(1, width), w_s, b_s.reshape(CHUNK, 1))


def _rope(x, cos, sin_signed, half):
    lane = lax.broadcasted_iota(jnp.int32, x.shape, 1)
    first = (lane & (2 * half - 1)) < half
    partner = jnp.where(first, pltpu.roll(x, LANE - half, axis=1), pltpu.roll(x, half, axis=1))
    return x * cos + partner * sin_signed


def _mla_pre_kernel(p_ref, qg_ref, kvg_ref, cq_ref, ckv_ref, *, q_lora, kv_lora):
    cq = p_ref[:, :q_lora]
    cq_ref[...] = (cq * lax.rsqrt(jnp.mean(cq * cq, axis=-1, keepdims=True) + EPS)
                   * qg_ref[...]).astype(cq_ref.dtype)
    ckv = p_ref[:, q_lora:q_lora + kv_lora]
    ckv_ref[...] = (ckv * lax.rsqrt(jnp.mean(ckv * ckv, axis=-1, keepdims=True) + EPS)
                    * kvg_ref[...]).astype(ckv_ref.dtype)


def _mla_pre(pb, q_gain, kv_gain, rows):
    m, n = pb.shape
    q_lora, kv_lora = q_gain.shape[0], kv_gain.shape[0]
    bm = rows.block(256)
    return pl.pallas_call(
        functools.partial(_mla_pre_kernel, q_lora=q_lora, kv_lora=kv_lora),
        grid=(m // bm,),
        in_specs=[pl.BlockSpec((bm, n), lambda i: (i, 0)),
                  pl.BlockSpec((1, q_lora), lambda i: (0, 0)),
                  pl.BlockSpec((1, kv_lora), lambda i: (0, 0))],
        out_specs=[pl.BlockSpec((bm, q_lora), lambda i: (i, 0)),
                   pl.BlockSpec((bm, kv_lora), lambda i: (i, 0))],
        out_shape=[jax.ShapeDtypeStruct((m, q_lora), BF16),
                   jax.ShapeDtypeStruct((m, kv_lora), BF16)],
        compiler_params=_cparams(("parallel",)),
        name="mla_pre",
    )(pb, q_gain.reshape(1, q_lora), kv_gain.reshape(1, kv_lora))


def _mla_post_kernel(q_ref, kv_ref, kr_ref, qgn_ref, qgr_ref, kgn_ref, kgr_ref, cos_ref, sin_ref,
                     qh_ref, kh_ref, vh_ref, *, heads):
    cos, sin = cos_ref[...], sin_ref[...]
    lane = lax.broadcasted_iota(jnp.int32, cos.shape, 1)
    low = lane < QK_ROPE
    half = QK_ROPE // 4
    zeros = jnp.zeros(cos.shape, F32)

    kr = jnp.where(low, kr_ref[...], 0.0)
    kr_ss = jnp.sum(kr * kr, axis=-1, keepdims=True)
    kr_rot = _rope(kr * kgr_ref[...], cos, sin, half)

    for hp in range(heads // 2):
        qr = q_ref[:, heads * QK_NOPE + hp * LANE: heads * QK_NOPE + (hp + 1) * LANE]
        qr2 = qr * qr
        ss_lo = jnp.sum(jnp.where(low, qr2, 0.0), axis=-1, keepdims=True)
        ss_hi = jnp.sum(jnp.where(low, 0.0, qr2), axis=-1, keepdims=True)
        rinv = []
        for j, ss_r in enumerate((ss_lo, ss_hi)):
            h = 2 * hp + j
            qn = q_ref[:, h * QK_NOPE:(h + 1) * QK_NOPE]
            r = lax.rsqrt((jnp.sum(qn * qn, axis=-1, keepdims=True) + ss_r) * (1.0 / B_QK) + EPS)
            rinv.append(r)
            qh_ref[h, :, :QK_NOPE] = (qn * r * qgn_ref[...]).astype(qh_ref.dtype)
        qrot = _rope(qr * jnp.where(low, rinv[0], rinv[1]) * qgr_ref[...], cos, sin, half)
        qh_ref[2 * hp, :, QK_NOPE:] = jnp.where(low, qrot, zeros).astype(qh_ref.dtype)
        qh_ref[2 * hp + 1, :, QK_NOPE:] = jnp.where(
            low, pltpu.roll(qrot, QK_ROPE, axis=1), zeros).astype(qh_ref.dtype)

    for h in range(heads):
        kn = kv_ref[:, h * QK_NOPE:(h + 1) * QK_NOPE]
        r = lax.rsqrt((jnp.sum(kn * kn, axis=-1, keepdims=True) + kr_ss) * (1.0 / B_QK) + EPS)
        kh_ref[h, :, :QK_NOPE] = (kn * r * kgn_ref[...]).astype(kh_ref.dtype)
        kh_ref[h, :, QK_NOPE:] = (kr_rot * r).astype(kh_ref.dtype)
        vh_ref[h] = kv_ref[:, (heads + h) * HEAD_DIM:(heads + h + 1) * HEAD_DIM].astype(vh_ref.dtype)


def _mla_post(q_raw, kv_raw, pb, kr_block, qn_gain, kn_gain, cos_t, sin_t, rows, heads):
    m = q_raw.shape[0]
    bm = rows.block(256)
    n_lat, n_tab = rows.m_lat // bm, rows.seq // bm
    tab = lambda i: (jnp.where(i < n_lat, i % n_tab, n_tab), 0)
    pair = lambda g: jnp.concatenate([g, g]).reshape(1, LANE)
    return pl.pallas_call(
        functools.partial(_mla_post_kernel, heads=heads),
        grid=(m // bm,),
        in_specs=[pl.BlockSpec((bm, q_raw.shape[1]), lambda i: (i, 0)),
                  pl.BlockSpec((bm, kv_raw.shape[1]), lambda i: (i, 0)),
                  pl.BlockSpec((bm, LANE), lambda i: (i, kr_block)),
                  pl.BlockSpec((1, QK_NOPE), lambda i: (0, 0)),
                  pl.BlockSpec((1, LANE), lambda i: (0, 0)),
                  pl.BlockSpec((1, QK_NOPE), lambda i: (0, 0)),
                  pl.BlockSpec((1, LANE), lambda i: (0, 0)),
                  pl.BlockSpec((bm, LANE), tab),
                  pl.BlockSpec((bm, LANE), tab)],
        out_specs=[pl.BlockSpec((heads, bm, B_QK_PAD), lambda i: (0, i, 0)),
                   pl.BlockSpec((heads, bm, B_QK_PAD), lambda i: (0, i, 0)),
                   pl.BlockSpec((heads, bm, HEAD_DIM), lambda i: (0, i, 0))],
        out_shape=[jax.ShapeDtypeStruct((heads, m, B_QK_PAD), BF16),
                   jax.ShapeDtypeStruct((heads, m, B_QK_PAD), BF16),
                   jax.ShapeDtypeStruct((heads, m, HEAD_DIM), BF16)],
        compiler_params=_cparams(("parallel",)),
        name="mla_post",
    )(q_raw, kv_raw, pb, qn_gain[:QK_NOPE].reshape(1, QK_NOPE), pair(qn_gain[QK_NOPE:]),
      kn_gain[:QK_NOPE].reshape(1, QK_NOPE), pair(kn_gain[QK_NOPE:]), cos_t, sin_t)


def _mla_attn_kernel(q_ref, kc_ref, vc_ref, *rest, with_lat):
    if with_lat:
        kl_ref, vl_ref, o_ref = rest
    else:
        (o_ref,) = rest
    scale = B_QK ** -0.5
    q = q_ref[...]
    s_c = _dot_nt(q, kc_ref[...]) * scale
    m = jnp.max(s_c, axis=-1, keepdims=True)
    if with_lat:
        s_l = _dot_nt(q, kl_ref[...]) * scale
        m = jnp.maximum(m, jnp.max(s_l, axis=-1, keepdims=True))
    p_c = jnp.exp(s_c - m)
    den = jnp.sum(p_c, axis=-1, keepdims=True)
    acc = jnp.dot(p_c.astype(BF16), vc_ref[...], preferred_element_type=F32)
    if with_lat:
        p_l = jnp.exp(s_l - m)
        den = den + jnp.sum(p_l, axis=-1, keepdims=True)
        acc = acc + jnp.dot(p_l.astype(BF16), vl_ref[...], preferred_element_type=F32)
    o_ref[...] = (acc / den).astype(o_ref.dtype)


def _mla_attention(qh, kh, vh, rows, heads, latent_queries):
    b, s, c = rows.batch, rows.seq, rows.ctx_len
    ctx_blk0 = rows.m_lat // c
    if latent_queries:
        tq = _pick(s, 512, 8)
        nq = s // tq
        q_map = lambda bi, h, i: (h, bi * nq + i, 0)
        o_map = lambda bi, h, i: (bi * nq + i, h)
        m_out = rows.m_lat
    else:
        tq, nq = c, 1
        q_map = lambda bi, h, i: (h, ctx_blk0 + bi, 0)
        o_map = lambda bi, h, i: (bi, h)
        m_out = rows.m_ctx
    in_specs = [pl.BlockSpec((None, tq, B_QK_PAD), q_map),
                pl.BlockSpec((None, c, B_QK_PAD), lambda bi, h, i: (h, ctx_blk0 + bi, 0)),
                pl.BlockSpec((None, c, HEAD_DIM), lambda bi, h, i: (h, ctx_blk0 + bi, 0))]
    args = [qh, kh, vh]
    if latent_queries:
        in_specs += [pl.BlockSpec((None, s, B_QK_PAD), lambda bi, h, i: (h, bi, 0)),
                     pl.BlockSpec((None, s, HEAD_DIM), lambda bi, h, i: (h, bi, 0))]
        args += [kh, vh]
    return pl.pallas_call(
        functools.partial(_mla_attn_kernel, with_lat=latent_queries),
        grid=(b, heads, nq),
        in_specs=in_specs,
        out_specs=pl.BlockSpec((tq, HEAD_DIM), o_map),
        out_shape=jax.ShapeDtypeStruct((m_out, heads * HEAD_DIM), BF16),
        compiler_params=_cparams(("parallel", "parallel", "parallel")),
        name="mla_attention",
    )(*args)


def _gqa_post_kernel(p_ref, qg_ref, kg_ref, cos_ref, sin_ref, q_ref, k_ref, v_ref, *, q_heads, kv_heads):
    cos, sin = cos_ref[...], sin_ref[...]
    half = HEAD_DIM // 4

    def norm_rope(x, gain):
        xn = x * lax.rsqrt(jnp.mean(x * x, axis=-1, keepdims=True) + EPS) * gain
        return _rope(xn, cos, sin, half)

    for h in range(q_heads):
        sl = slice(h * HEAD_DIM, (h + 1) * HEAD_DIM)
        q_ref[:, sl] = norm_rope(p_ref[:, sl], qg_ref[...]).astype(q_ref.dtype)
    for h in range(kv_heads):
        sl = slice(h * HEAD_DIM, (h + 1) * HEAD_DIM)
        k0 = q_heads * HEAD_DIM
        v0 = (q_heads + kv_heads) * HEAD_DIM
        k_ref[:, sl] = norm_rope(p_ref[:, k0 + h * HEAD_DIM:k0 + (h + 1) * HEAD_DIM],
                                 kg_ref[...]).astype(k_ref.dtype)
        v_ref[:, sl] = p_ref[:, v0 + h * HEAD_DIM:v0 + (h + 1) * HEAD_DIM].astype(v_ref.dtype)


def _gqa_post(pc, qn_gain, kn_gain, cos_t, sin_t, rows, q_heads, kv_heads):
    m, n = pc.shape
    bm = rows.block(256)
    n_lat, n_tab = rows.m_lat // bm, rows.seq // bm
    tab = lambda i: (jnp.where(i < n_lat, i % n_tab, n_tab), 0)
    return pl.pallas_call(
        functools.partial(_gqa_post_kernel, q_heads=q_heads, kv_heads=kv_heads),
        grid=(m // bm,),
        in_specs=[pl.BlockSpec((bm, n), lambda i: (i, 0)),
                  pl.BlockSpec((1, HEAD_DIM), lambda i: (0, 0)),
                  pl.BlockSpec((1, HEAD_DIM), lambda i: (0, 0)),
                  pl.BlockSpec((bm, LANE), tab),
                  pl.BlockSpec((bm, LANE), tab)],
        out_specs=[pl.BlockSpec((bm, q_heads * HEAD_DIM), lambda i: (i, 0)),
                   pl.BlockSpec((bm, kv_heads * HEAD_DIM), lambda i: (i, 0)),
                   pl.BlockSpec((bm, kv_heads * HEAD_DIM), lambda i: (i, 0))],
        out_shape=[jax.ShapeDtypeStruct((m, q_heads * HEAD_DIM), BF16),
                   jax.ShapeDtypeStruct((m, kv_heads * HEAD_DIM), BF16),
                   jax.ShapeDtypeStruct((m, kv_heads * HEAD_DIM), BF16)],
        compiler_params=_cparams(("parallel",)),
        name="gqa_post",
    )(pc, qn_gain.reshape(1, HEAD_DIM), kn_gain.reshape(1, HEAD_DIM), cos_t, sin_t)


def _gqa_attn_kernel(sink_ref, q_ref, kc_ref, vc_ref, *rest, group, seq, banded):
    if banded:
        kp_ref, kq_ref, kn_ref, vp_ref, vq_ref, vn_ref, o_ref = rest
    else:
        (o_ref,) = rest
    scale = HEAD_DIM ** -0.5
    kvh = pl.program_id(1)
    blk = pl.program_id(2)
    rows_q = group * WINDOW
    q = jnp.concatenate([q_ref[:, g * HEAD_DIM:(g + 1) * HEAD_DIM] for g in range(group)], axis=0)

    row = lax.broadcasted_iota(jnp.int32, (rows_q, 1), 0)
    sink = jnp.zeros((rows_q, 1), F32)
    for g in range(group):
        in_g = (row >= g * WINDOW) & (row < (g + 1) * WINDOW)
        sink = jnp.where(in_g, sink_ref[kvh * group + g], sink)

    s_c = _dot_nt(q, kc_ref[...]) * scale
    m = jnp.maximum(jnp.max(s_c, axis=-1, keepdims=True), sink)
    if banded:
        kb = jnp.concatenate([kp_ref[...], kq_ref[...], kn_ref[...]], axis=0)
        vb = jnp.concatenate([vp_ref[...], vq_ref[...], vn_ref[...]], axis=0)
        s_b = _dot_nt(q, kb) * scale
        qpos = lax.broadcasted_iota(jnp.int32, s_b.shape, 0) & (WINDOW - 1)
        krel = lax.broadcasted_iota(jnp.int32, s_b.shape, 1) - WINDOW
        kpos = blk * WINDOW + krel
        valid = (jnp.abs(qpos - krel) <= WINDOW) & (kpos >= 0) & (kpos < seq)
        s_b = jnp.where(valid, s_b, NEG)
        m = jnp.maximum(m, jnp.max(s_b, axis=-1, keepdims=True))
    p_c = jnp.exp(s_c - m)
    den = jnp.sum(p_c, axis=-1, keepdims=True) + jnp.exp(sink - m)
    acc = jnp.dot(p_c.astype(BF16), vc_ref[...], preferred_element_type=F32)
    if banded:
        p_b = jnp.exp(s_b - m)
        den = den + jnp.sum(p_b, axis=-1, keepdims=True)
        acc = acc + jnp.dot(p_b.astype(BF16), vb, preferred_element_type=F32)
    o = acc / den
    for g in range(group):
        o_ref[:, g * HEAD_DIM:(g + 1) * HEAD_DIM] = o[g * WINDOW:(g + 1) * WINDOW].astype(o_ref.dtype)


def _gqa_attention(qc, kc, vc, sink, rows, q_heads, kv_heads, latent_queries):
    b, s, c = rows.batch, rows.seq, rows.ctx_len
    group = q_heads // kv_heads
    gw = group * HEAD_DIM
    ctx_blk0 = rows.m_lat // c
    ctx_spec = lambda: pl.BlockSpec((c, HEAD_DIM), lambda bi, k, n: (ctx_blk0 + bi, k))
    if latent_queries:
        nb = s // WINDOW
        qrow = lambda bi, n: bi * nb + n
        m_out = rows.m_lat
        orow = qrow
    else:
        nb = c // WINDOW
        qrow = lambda bi, n: rows.m_lat // WINDOW + bi * nb + n
        orow = lambda bi, n: bi * nb + n
        m_out = rows.m_ctx
    in_specs = [pl.BlockSpec(memory_space=pltpu.SMEM),
                pl.BlockSpec((WINDOW, gw), lambda bi, k, n: (qrow(bi, n), k)),
                ctx_spec(), ctx_spec()]
    args = [sink, qc, kc, vc]
    if latent_queries:
        band = [lambda bi, k, n: (bi * nb + jnp.maximum(n - 1, 0), k),
                lambda bi, k, n: (bi * nb + n, k),
                lambda bi, k, n: (bi * nb + jnp.minimum(n + 1, nb - 1), k)]
        in_specs += [pl.BlockSpec((WINDOW, HEAD_DIM), f) for f in band] * 2
        args += [kc, kc, kc, vc, vc, vc]
    return pl.pallas_call(
        functools.partial(_gqa_attn_kernel, group=group, seq=s, banded=latent_queries),
        grid=(b, kv_heads, nb),
        in_specs=in_specs,
        out_specs=pl.BlockSpec((WINDOW, gw), lambda bi, k, n: (orow(bi, n), k)),
        out_shape=jax.ShapeDtypeStruct((m_out, q_heads * HEAD_DIM), BF16),
        compiler_params=_cparams(("parallel", "parallel", "parallel")),
        name="gqa_attention",
    )(*args)


def _extract_topk(s, pos, ids, count, val_ref, id_ref):
    beyond = 3.0e38
    for r in range(count):
        top = jnp.max(s, axis=0, keepdims=True)
        first = jnp.min(jnp.where(s == top, pos, beyond), axis=0, keepdims=True)
        hit = pos == first
        val_ref[r:r + 1, :] = top
        if ids is None:
            id_ref[r:r + 1, :] = first
        else:
            id_ref[r:r + 1, :] = jnp.max(jnp.where(hit, ids, -1.0), axis=0, keepdims=True)
        s = jnp.where(hit, -jnp.inf, s)


_PAIR_SPANS = [(a, PEER_TOPK // (a + 1)) for a in range(PEER_TOPK)]
_N_PAIRS = sum(n for _, n in _PAIR_SPANS)
_N_PAIRS_PAD = -(-_N_PAIRS // 8) * 8


def _route_kernel(q_ref, sk_ref, idx_ref, gate_ref, tv_ref, ti_ref, cs_ref, cp_ref, ci_ref, bs_ref,
                  bi_ref, *, n_keys):
    tt = q_ref.shape[0]
    key_id = lax.broadcasted_iota(jnp.int32, (n_keys, tt), 0).astype(F32)
    for p in range(2):
        qp = q_ref[:, p * LANE:(p + 1) * LANE].astype(BF16)
        scores = _dot_nt(sk_ref[p], qp)
        _extract_topk(scores, key_id, None, PEER_TOPK, tv_ref.at[p], ti_ref.at[p])
    cs_ref[...] = jnp.full(cs_ref.shape, -jnp.inf, F32)
    cp_ref[...] = jnp.full(cp_ref.shape, 1.0e6, F32)
    ci_ref[...] = jnp.full(ci_ref.shape, -1.0, F32)
    off = 0
    for a, n in _PAIR_SPANS:
        rs = slice(off, off + n)
        cs_ref[rs, :] = tv_ref[0, a:a + 1, :] + tv_ref[1, 0:n, :]
        cp_ref[rs, :] = float(a * PEER_TOPK) + lax.broadcasted_iota(jnp.int32, (n, tt), 0).astype(F32)
        ci_ref[rs, :] = ti_ref[0, a:a + 1, :] * float(n_keys) + ti_ref[1, 0:n, :]
        off += n
    _extract_topk(cs_ref[...], cp_ref[...], ci_ref[...], PEER_TOPK, bs_ref, bi_ref)
    best = bs_ref[...]
    e = jnp.exp(best - best[0:1, :])
    gate_ref[...] = e / jnp.sum(e, axis=0, keepdims=True)
    idx_ref[...] = bi_ref[...].astype(jnp.int32)


def _peer_route(pq, subkeys):
    m = pq.shape[0]
    heads, _, n_keys, half = subkeys.shape
    assert half == LANE
    tt = _pick(m, 256)
    k = PEER_TOPK
    return pl.pallas_call(
        functools.partial(_route_kernel, n_keys=n_keys),
        grid=(m // tt, heads),
        in_specs=[pl.BlockSpec((tt, 2 * LANE), lambda i, h: (i, h)),
                  pl.BlockSpec((None, 2, n_keys, half), lambda i, h: (h, 0, 0, 0))],
        out_specs=[pl.BlockSpec((None, k, tt), lambda i, h: (h, 0, i)),
                   pl.BlockSpec((None, k, tt), lambda i, h: (h, 0, i))],
        out_shape=[jax.ShapeDtypeStruct((heads, k, m), jnp.int32),
                   jax.ShapeDtypeStruct((heads, k, m), F32)],
        scratch_shapes=[pltpu.VMEM((2, k, tt), F32), pltpu.VMEM((2, k, tt), F32)]
        + [pltpu.VMEM((_N_PAIRS_PAD, tt), F32)] * 3
        + [pltpu.VMEM((k, tt), F32), pltpu.VMEM((k, tt), F32)],
        compiler_params=_cparams(("parallel", "parallel")),
        name="peer_route",
    )(pq, subkeys.astype(BF16))


HI16 = 0xFFFF0000


def _pack_kernel(u_ref, v_ref, o_ref):
    ub = lax.bitcast_convert_type(u_ref[...].astype(BF16).astype(F32), jnp.uint32)
    vb = lax.bitcast_convert_type(v_ref[...].astype(BF16).astype(F32), jnp.uint32)
    o_ref[...] = (vb & jnp.uint32(HI16)) | (ub >> 16)


def _pack_tables(u_all, v_all, layer):
    _, e, d = u_all.shape
    be = _pick(e, 256, 8)
    spec = pl.BlockSpec((None, be, d), lambda i: (layer, i, 0))
    return pl.pallas_call(
        _pack_kernel,
        grid=(e // be,),
        in_specs=[spec, spec],
        out_specs=pl.BlockSpec((be, d), lambda i: (i, 0)),
        out_shape=jax.ShapeDtypeStruct((e, d), jnp.uint32),
        compiler_params=_cparams(("parallel",)),
        name="peer_pack",
    )(u_all, v_all)


def _peer_kernel(idx_ref, g_ref, gate_ref, tab_hbm, o_ref, buf_a, buf_b, sem, *, tokens, n_sel):
    step = pl.program_id(0)
    tile = 8

    def row_copy(e, buf, k, s):
        return pltpu.make_async_copy(tab_hbm.at[pl.ds(e, 1), :], buf.at[pl.ds(k, 1), :], sem.at[s])

    def wait(buf, s):
        pltpu.make_async_copy(tab_hbm.at[pl.ds(0, n_sel), :], buf, sem.at[s]).wait()

    @pl.when(step == 0)
    def _():
        for k in range(n_sel):
            row_copy(idx_ref[0, k], buf_a, k, 0).start()

    eye = (lax.broadcasted_iota(jnp.int32, (n_sel, n_sel), 0)
           == lax.broadcasted_iota(jnp.int32, (n_sel, n_sel), 1))

    def process(t, cur, s_cur, nxt, s_nxt):
        wait(cur, s_cur)
        g = g_ref[pl.ds(t, 1), :]
        parts = []
        for c in range(n_sel // tile):
            for k in range(c * tile, (c + 1) * tile):
                row_copy(idx_ref[t + 1, k], nxt, k, s_nxt).start()
            u = lax.bitcast_convert_type(cur[c * tile:(c + 1) * tile, :] << 16, F32)
            parts.append(jnp.sum(u * g, axis=1, keepdims=True))
        d = jnp.concatenate(parts, axis=0)
        gate_row = gate_ref[pl.ds(t, 1), :]
        gate_col = jnp.sum(jnp.where(eye, gate_row, 0.0), axis=1, keepdims=True)
        w = gate_col * _gelu(d)
        acc = jnp.zeros((tile, g.shape[1]), F32)
        for c in range(n_sel // tile):
            v = lax.bitcast_convert_type(cur[c * tile:(c + 1) * tile, :] & jnp.uint32(HI16), F32)
            acc = acc + v * w[c * tile:(c + 1) * tile, :]
        o_ref[pl.ds(t, 1), :] = jnp.sum(acc, axis=0, keepdims=True)

    def pair(i, carry):
        process(2 * i, buf_a, 0, buf_b, 1)
        process(2 * i + 1, buf_b, 1, buf_a, 0)
        return carry

    lax.fori_loop(0, tokens // 2, pair, 0)

    @pl.when(step == pl.num_programs(0) - 1)
    def _():
        wait(buf_a, 0)


def _peer_experts(idx, gate, g, tab):
    m, d = g.shape
    n_sel = idx.shape[1]
    tb = _pick(m, 128, 8)
    nb = m // tb
    assert tb % 2 == 0
    blocks = idx.reshape(nb, tb, n_sel)
    nxt = jnp.roll(blocks[:, :1], -1, axis=0)
    idx_ext = jnp.concatenate([blocks, nxt, jnp.zeros((nb, 7, n_sel), idx.dtype)], axis=1)
    return pl.pallas_call(
        functools.partial(_peer_kernel, tokens=tb, n_sel=n_sel),
        grid=(nb,),
        in_specs=[pl.BlockSpec((None, tb + 8, n_sel), lambda i: (i, 0, 0), memory_space=pltpu.SMEM),
                  pl.BlockSpec((tb, d), lambda i: (i, 0)),
                  pl.BlockSpec((tb, n_sel), lambda i: (i, 0)),
                  pl.BlockSpec(memory_space=pl.ANY)],
        out_specs=pl.BlockSpec((tb, d), lambda i: (i, 0)),
        out_shape=jax.ShapeDtypeStruct((m, d), F32),
        scratch_shapes=[pltpu.VMEM((n_sel, d), jnp.uint32), pltpu.VMEM((n_sel, d), jnp.uint32),
                        pltpu.SemaphoreType.DMA((2,))],
        compiler_params=_cparams(("arbitrary",)),
        name="peer_experts",
    )(idx_ext, g, gate, tab)


def _rope_tables(seq, rot_dim, pad):
    rows = seq // GRID_W
    row = jnp.repeat(jnp.arange(rows, dtype=F32), GRID_W)
    col = jnp.tile(jnp.arange(GRID_W, dtype=F32), rows)
    n_freq = rot_dim // 4
    inv = ROPE_BASE ** (-jnp.arange(n_freq, dtype=F32) / n_freq)
    ar, ac = row[:, None] * inv, col[:, None] * inv
    ang = jnp.concatenate([ar, ar, ac, ac], axis=-1)
    sign = jnp.concatenate([-jnp.ones(n_freq), jnp.ones(n_freq)] * 2).astype(F32)
    cos, sin = jnp.cos(ang), jnp.sin(ang) * sign
    reps = LANE // rot_dim
    cos, sin = jnp.tile(cos, (1, reps)), jnp.tile(sin, (1, reps))
    cos = jnp.concatenate([cos, jnp.ones((pad, LANE), F32)], axis=0)
    sin = jnp.concatenate([sin, jnp.zeros((pad, LANE), F32)], axis=0)
    return cos, sin


def kernel(x, c, ctx, c_ctx, w_mod, b_mod, norm1_gain, norm2_gain, w_in, a_v_gain, a_w_s, a_b_s,
           b_q_gain, b_kv_gain, b_w_uq, b_w_ukv, b_qn_gain, b_kn_gain, c_qn_gain, c_kn_gain, c_sink,
           w_out, peer_w_q, peer_subkeys, peer_u, peer_v):
    batch, seq, d = x.shape
    ctx_len = ctx.shape[1]
    depth = w_mod.shape[0]
    a_width = a_v_gain.shape[1]
    q_lora, kv_lora = b_q_gain.shape[1], b_kv_gain.shape[1]
    b_heads = b_w_uq.shape[2] // B_QK
    c_heads = c_sink.shape[1]
    c_kv = c_heads // 3
    p_a = 2 * a_width
    p_b = q_lora + kv_lora + QK_ROPE
    p_b_pad = -(-p_b // (2 * LANE)) * (2 * LANE)
    assert b_heads % 2 == 0 and (q_lora + kv_lora) % LANE == 0
    assert seq % WINDOW == 0 and ctx_len % WINDOW == 0

    rows_all = _Rows(batch, seq, ctx_len)
    bm_tab = rows_all.block(256)
    cos_b, sin_b = _rope_tables(seq, QK_ROPE, bm_tab)
    cos_c, sin_c = _rope_tables(seq, HEAD_DIM, bm_tab)

    c8 = jnp.zeros((8, d), F32).at[:batch].set(c).at[batch].set(c_ctx)
    xs = jnp.concatenate([x.reshape(batch * seq, d), ctx.reshape(batch * ctx_len, d)], axis=0)

    for layer in range(depth):
        need_ctx = layer < depth - 1
        mod3 = _modulation(c8, w_mod, b_mod[layer], layer).reshape(8, 1, N_MOD * d)

        w_l = w_in[layer]
        w_a = w_l[:, :p_a].astype(BF16)
        w_b = jnp.pad(w_l[:, p_a:p_a + p_b], ((0, 0), (0, p_b_pad - p_b))).astype(BF16)
        w_c = w_l[:, p_a + p_b:].astype(BF16)
        w_uq = b_w_uq[layer].reshape(q_lora, b_heads, B_QK)
        w_uq = jnp.concatenate([w_uq[:, :, :QK_NOPE].reshape(q_lora, -1),
                                w_uq[:, :, QK_NOPE:].reshape(q_lora, -1)], axis=1).astype(BF16)
        w_ukv = b_w_ukv[layer].reshape(kv_lora, b_heads, QK_NOPE + HEAD_DIM)
        w_ukv = jnp.concatenate([w_ukv[:, :, :QK_NOPE].reshape(kv_lora, -1),
                                 w_ukv[:, :, QK_NOPE:].reshape(kv_lora, -1)], axis=1).astype(BF16)

        (h,) = _norm_mod(xs, norm1_gain[layer], mod3, 0, 1, rows_all, [BF16])
        pa = _matmul(h, w_a)
        pb = _matmul(h, w_b)
        pc = _matmul(h, w_c)

        oa = _gmlp(pa, a_v_gain[layer], a_w_s[layer], a_b_s[layer])

        cq, ckv = _mla_pre(pb, b_q_gain[layer], b_kv_gain[layer], rows_all)
        q_raw = _matmul(cq, w_uq)
        kv_raw = _matmul(ckv, w_ukv)
        qh, kh, vh = _mla_post(q_raw, kv_raw, pb, (q_lora + kv_lora) // LANE, b_qn_gain[layer],
                               b_kn_gain[layer], cos_b, sin_b, rows_all, b_heads)
        ob = _mla_attention(qh, kh, vh, rows_all, b_heads, True)

        qc, kc, vc = _gqa_post(pc, c_qn_gain[layer], c_kn_gain[layer], cos_c, sin_c, rows_all,
                               c_heads, c_kv)
        oc = _gqa_attention(qc, kc, vc, c_sink[layer], rows_all, c_heads, c_kv, True)

        if need_ctx:
            rows = rows_all
            ob = jnp.concatenate([ob, _mla_attention(qh, kh, vh, rows_all, b_heads, False)], axis=0)
            oc = jnp.concatenate(
                [oc, _gqa_attention(qc, kc, vc, c_sink[layer], rows_all, c_heads, c_kv, False)], axis=0)
        else:
            rows = _Rows(batch, seq, 0)
            oa = oa[:rows.m_lat]
            xs = xs[:rows.m_lat]
        o = jnp.concatenate([oa, ob, oc], axis=1)
        xs = _matmul_residual(o, w_out[layer].astype(BF16), xs, mod3, 2, rows)

        g_bf, g = _norm_mod(xs, norm2_gain[layer], mod3, 3, 4, rows, [BF16, F32])
        pq = _matmul(g_bf, peer_w_q[layer].astype(BF16))
        idx, gate = _peer_route(pq, peer_subkeys[layer])
        n_sel = idx.shape[0] * idx.shape[1]
        idx = idx.reshape(n_sel, -1).T
        gate = gate.reshape(n_sel, -1).T
        y = _peer_experts(idx, gate, g, _pack_tables(peer_u, peer_v, layer))
        xs = _gated_add(xs, y, mod3, 5, rows)

    return xs[:batch * seq].reshape(batch, seq, d)
```

```python
import functools

import jax
import jax.numpy as jnp
from jax import lax
from jax.experimental import pallas as pl
from jax.experimental.pallas import tpu as pltpu

F32 = jnp.float32
BF16 = jnp.bfloat16

LANE = 128
HEAD_DIM = 128
QK_NOPE = 128
QK_ROPE = 64
B_QK = QK_NOPE + QK_ROPE
B_QK_PAD = 2 * LANE
GRID_W = 64
WINDOW = 128
CHUNK = 128
PEER_TOPK = 16
N_MOD = 6
ROPE_BASE = 10000.0
EPS = 1e-6
NEG = -1e30
MIB = 1024 * 1024
VMEM_LIMIT = 56 * MIB


def _cparams(semantics, vmem=VMEM_LIMIT):
    return pltpu.CompilerParams(dimension_semantics=semantics, vmem_limit_bytes=vmem)


def _pick(n, target, align=LANE):
    if n <= target:
        return n
    best = None
    d = align
    while d <= target:
        if n % d == 0:
            best = d
        d += align
    assert best is not None, (n, target, align)
    return best


def _gelu(x):
    return 0.5 * x * (1.0 + jnp.tanh(0.7978845608028654 * (x + 0.044715 * (x * x * x))))


def _dot_nt(a, b):
    return lax.dot_general(a, b, (((1,), (1,)), ((), ())), preferred_element_type=F32)


def _modulation_kernel(c_ref, w_ref, b_ref, o_ref, acc_ref):
    k = pl.program_id(1)

    @pl.when(k == 0)
    def _():
        acc_ref[...] = jnp.zeros_like(acc_ref)

    c = c_ref[...]
    silu = c / (1.0 + jnp.exp(-c))
    acc_ref[...] += jnp.dot(silu.astype(BF16), w_ref[...].astype(BF16), preferred_element_type=F32)

    @pl.when(k == pl.num_programs(1) - 1)
    def _():
        o_ref[...] = acc_ref[...] + b_ref[...]


def _modulation(c8, w_all, b, layer):
    rows, d = c8.shape
    n = w_all.shape[2]
    bn = _pick(n, 2048)
    bk = _pick(d, 1024)
    return pl.pallas_call(
        _modulation_kernel,
        grid=(n // bn, d // bk),
        in_specs=[pl.BlockSpec((rows, bk), lambda j, k: (0, k)),
                  pl.BlockSpec((None, bk, bn), lambda j, k: (layer, k, j)),
                  pl.BlockSpec((1, bn), lambda j, k: (0, j))],
        out_specs=pl.BlockSpec((rows, bn), lambda j, k: (0, j)),
        out_shape=jax.ShapeDtypeStruct((rows, n), F32),
        scratch_shapes=[pltpu.VMEM((rows, bn), F32)],
        compiler_params=_cparams(("parallel", "arbitrary")),
        name="modulation",
    )(c8, w_all, b.reshape(1, n))


class _Rows:
    def __init__(self, batch, seq, ctx_len):
        self.batch, self.seq, self.ctx_len = batch, seq, ctx_len
        self.m_lat = batch * seq
        self.m_ctx = batch * ctx_len
        self.m = self.m_lat + self.m_ctx

    def block(self, target):
        bm = target
        while self.seq % bm or self.m_ctx % bm:
            bm //= 2
        assert bm >= 8
        return bm

    def mod_row(self, i, bm):
        return jnp.where(i < self.m_lat // bm, (i * bm) // self.seq, self.batch)


def _norm_mod_kernel(x_ref, gain_ref, shift_ref, scale_ref, *out_refs):
    x = x_ref[...]
    y = x * lax.rsqrt(jnp.mean(x * x, axis=-1, keepdims=True) + EPS) * gain_ref[...]
    h = y * (1.0 + scale_ref[0]) + shift_ref[0]
    for o in out_refs:
        o[...] = h.astype(o.dtype)


def _norm_mod(x, gain, mod3, shift_chunk, scale_chunk, rows, out_dtypes):
    m, d = x.shape
    bm = rows.block(256)
    outs = pl.pallas_call(
        _norm_mod_kernel,
        grid=(m // bm,),
        in_specs=[pl.BlockSpec((bm, d), lambda i: (i, 0)),
                  pl.BlockSpec((1, d), lambda i: (0, 0)),
                  pl.BlockSpec((1, 1, d), lambda i: (rows.mod_row(i, bm), 0, shift_chunk)),
                  pl.BlockSpec((1, 1, d), lambda i: (rows.mod_row(i, bm), 0, scale_chunk))],
        out_specs=[pl.BlockSpec((bm, d), lambda i: (i, 0)) for _ in out_dtypes],
        out_shape=[jax.ShapeDtypeStruct((m, d), dt) for dt in out_dtypes],
        compiler_params=_cparams(("parallel",)),
        name="norm_mod",
    )(x, gain.reshape(1, d), mod3, mod3)
    return outs


def _mm_kernel(a_ref, w_ref, o_ref):
    o_ref[...] = jnp.dot(a_ref[...], w_ref[...], preferred_element_type=F32).astype(o_ref.dtype)


def _mm_res_kernel(a_ref, w_ref, res_ref, gate_ref, o_ref):
    acc = jnp.dot(a_ref[...], w_ref[...], preferred_element_type=F32)
    o_ref[...] = res_ref[...] + gate_ref[0] * acc


def _matmul(a, w, out_dtype=F32, bm_target=1024, bn_target=1024):
    m, k = a.shape
    n = w.shape[1]
    bm = _pick(m, bm_target, 8)
    bn = _pick(n, bn_target)
    return pl.pallas_call(
        _mm_kernel,
        grid=(m // bm, n // bn),
        in_specs=[pl.BlockSpec((bm, k), lambda i, j: (i, 0)),
                  pl.BlockSpec((k, bn), lambda i, j: (0, j))],
        out_specs=pl.BlockSpec((bm, bn), lambda i, j: (i, j)),
        out_shape=jax.ShapeDtypeStruct((m, n), out_dtype),
        compiler_params=_cparams(("parallel", "parallel")),
        name="matmul",
    )(a, w)


def _matmul_residual(a, w, res, mod3, gate_chunk, rows):
    m, k = a.shape
    n = w.shape[1]
    bm = rows.block(1024)
    bn = _pick(n, 512)
    return pl.pallas_call(
        _mm_res_kernel,
        grid=(m // bm, n // bn),
        in_specs=[pl.BlockSpec((bm, k), lambda i, j: (i, 0)),
                  pl.BlockSpec((k, bn), lambda i, j: (0, j)),
                  pl.BlockSpec((bm, bn), lambda i, j: (i, j)),
                  pl.BlockSpec((1, 1, bn),
                               lambda i, j: (rows.mod_row(i, bm), 0, gate_chunk * (n // bn) + j))],
        out_specs=pl.BlockSpec((bm, bn), lambda i, j: (i, j)),
        out_shape=jax.ShapeDtypeStruct((m, n), F32),
        compiler_params=_cparams(("parallel", "parallel")),
        name="matmul_residual",
    )(a, w, res, mod3)


def _gated_add_kernel(res_ref, y_ref, gate_ref, o_ref):
    o_ref[...] = res_ref[...] + gate_ref[0] * y_ref[...]


def _gated_add(res, y, mod3, gate_chunk, rows):
    m, d = res.shape
    bm = rows.block(256)
    return pl.pallas_call(
        _gated_add_kernel,
        grid=(m // bm,),
        in_specs=[pl.BlockSpec((bm, d), lambda i: (i, 0)),
                  pl.BlockSpec((bm, d), lambda i: (i, 0)),
                  pl.BlockSpec((1, 1, d), lambda i: (rows.mod_row(i, bm), 0, gate_chunk))],
        out_specs=pl.BlockSpec((bm, d), lambda i: (i, 0)),
        out_shape=jax.ShapeDtypeStruct((m, d), F32),
        compiler_params=_cparams(("parallel",)),
        name="gated_add",
    )(res, y, mod3)


def _gmlp_kernel(p_ref, gain_ref, ws_ref, bs_ref, o_ref, *, width, chunks):
    ws = ws_ref[...].astype(BF16)
    bs = bs_ref[...]
    for c in range(chunks):
        r0 = c * CHUNK
        z = _gelu(p_ref[r0:r0 + CHUNK, :])
        u = z[:, :width]
        parts = []
        for h in range(width // HEAD_DIM):
            vh = z[:, width + h * HEAD_DIM: width + (h + 1) * HEAD_DIM]
            vh = vh * lax.rsqrt(jnp.mean(vh * vh, axis=-1, keepdims=True) + EPS)
            parts.append((vh * gain_ref[:, h * HEAD_DIM:(h + 1) * HEAD_DIM]).astype(BF16))
        vn = jnp.concatenate(parts, axis=1)
        s = jnp.dot(ws, vn, preferred_element_type=F32) + bs
        o_ref[r0:r0 + CHUNK, :] = (u * s).astype(o_ref.dtype)


def _gmlp(pa, v_gain, w_s, b_s):
    m, two_w = pa.shape
    width = two_w // 2
    chunks = 2
    bm = chunks * CHUNK
    return pl.pallas_call(
        functools.partial(_gmlp_kernel, width=width, chunks=chunks),
        grid=(m // bm,),
        in_specs=[pl.BlockSpec((bm, two_w), lambda i: (i, 0)),
                  pl.BlockSpec((1, width), lambda i: (0, 0)),
                  pl.BlockSpec((CHUNK, CHUNK), lambda i: (0, 0)),
                  pl.BlockSpec((CHUNK, 1), lambda i: (0, 0))],
        out_specs=pl.BlockSpec((bm, width), lambda i: (i, 0)),
        out_shape=jax.ShapeDtypeStruct((m, width), BF16),
        compiler_params=_cparams(("parallel",)),
        name="gmlp",
    )(pa, v_gain.reshape(1, width), w_s, b_s.reshape(CHUNK, 1))


def _rope(x, cos, sin_signed, half):
    lane = lax.broadcasted_iota(jnp.int32, x.shape, 1)
    first = (lane & (2 * half - 1)) < half
    partner = jnp.where(first, pltpu.roll(x, LANE - half, axis=1), pltpu.roll(x, half, axis=1))
    return x * cos + partner * sin_signed


def _mla_pre_kernel(p_ref, qg_ref, kvg_ref, cq_ref, ckv_ref, *, q_lora, kv_lora):
    cq = p_ref[:, :q_lora]
    cq_ref[...] = (cq * lax.rsqrt(jnp.mean(cq * cq, axis=-1, keepdims=True) + EPS)
                   * qg_ref[...]).astype(cq_ref.dtype)
    ckv = p_ref[:, q_lora:q_lora + kv_lora]
    ckv_ref[...] = (ckv * lax.rsqrt(jnp.mean(ckv * ckv, axis=-1, keepdims=True) + EPS)
                    * kvg_ref[...]).astype(ckv_ref.dtype)


def _mla_pre(pb, q_gain, kv_gain, rows):
    m, n = pb.shape
    q_lora, kv_lora = q_gain.shape[0], kv_gain.shape[0]
    bm = rows.block(256)
    return pl.pallas_call(
        functools.partial(_mla_pre_kernel, q_lora=q_lora, kv_lora=kv_lora),
        grid=(m // bm,),
        in_specs=[pl.BlockSpec((bm, n), lambda i: (i, 0)),
                  pl.BlockSpec((1, q_lora), lambda i: (0, 0)),
                  pl.BlockSpec((1, kv_lora), lambda i: (0, 0))],
        out_specs=[pl.BlockSpec((bm, q_lora), lambda i: (i, 0)),
                   pl.BlockSpec((bm, kv_lora), lambda i: (i, 0))],
        out_shape=[jax.ShapeDtypeStruct((m, q_lora), BF16),
                   jax.ShapeDtypeStruct((m, kv_lora), BF16)],
        compiler_params=_cparams(("parallel",)),
        name="mla_pre",
    )(pb, q_gain.reshape(1, q_lora), kv_gain.reshape(1, kv_lora))


def _mla_post_kernel(q_ref, kv_ref, kr_ref, qgn_ref, qgr_ref, kgn_ref, kgr_ref, cos_ref, sin_ref,
                     qh_ref, kh_ref, vh_ref, *, heads):
    cos, sin = cos_ref[...], sin_ref[...]
    lane = lax.broadcasted_iota(jnp.int32, cos.shape, 1)
    low = lane < QK_ROPE
    half = QK_ROPE // 4
    zeros = jnp.zeros(cos.shape, F32)

    kr = jnp.where(low, kr_ref[...], 0.0)
    kr_ss = jnp.sum(kr * kr, axis=-1, keepdims=True)
    kr_rot = _rope(kr * kgr_ref[...], cos, sin, half)

    for hp in range(heads // 2):
        qr = q_ref[:, heads * QK_NOPE + hp * LANE: heads * QK_NOPE + (hp + 1) * LANE]
        qr2 = qr * qr
        ss_lo = jnp.sum(jnp.where(low, qr2, 0.0), axis=-1, keepdims=True)
        ss_hi = jnp.sum(jnp.where(low, 0.0, qr2), axis=-1, keepdims=True)
        rinv = []
        for j, ss_r in enumerate((ss_lo, ss_hi)):
            h = 2 * hp + j
            qn = q_ref[:, h * QK_NOPE:(h + 1) * QK_NOPE]
            r = lax.rsqrt((jnp.sum(qn * qn, axis=-1, keepdims=True) + ss_r) * (1.0 / B_QK) + EPS)
            rinv.append(r)
            qh_ref[h, :, :QK_NOPE] = (qn * r * qgn_ref[...]).astype(qh_ref.dtype)
        qrot = _rope(qr * jnp.where(low, rinv[0], rinv[1]) * qgr_ref[...], cos, sin, half)
        qh_ref[2 * hp, :, QK_NOPE:] = jnp.where(low, qrot, zeros).astype(qh_ref.dtype)
        qh_ref[2 * hp + 1, :, QK_NOPE:] = jnp.where(
            low, pltpu.roll(qrot, QK_ROPE, axis=1), zeros).astype(qh_ref.dtype)

    for h in range(heads):
        kn = kv_ref[:, h * QK_NOPE:(h + 1) * QK_NOPE]
        r = lax.rsqrt((jnp.sum(kn * kn, axis=-1, keepdims=True) + kr_ss) * (1.0 / B_QK) + EPS)
        kh_ref[h, :, :QK_NOPE] = (kn * r * kgn_ref[...]).astype(kh_ref.dtype)
        kh_ref[h, :, QK_NOPE:] = (kr_rot * r).astype(kh_ref.dtype)
        vh_ref[h] = kv_ref[:, (heads + h) * HEAD_DIM:(heads + h + 1) * HEAD_DIM].astype(vh_ref.dtype)


def _mla_post(q_raw, kv_raw, pb, kr_block, qn_gain, kn_gain, cos_t, sin_t, rows, heads):
    m = q_raw.shape[0]
    bm = rows.block(256)
    n_lat, n_tab = rows.m_lat // bm, rows.seq // bm
    tab = lambda i: (jnp.where(i < n_lat, i % n_tab, n_tab), 0)
    pair = lambda g: jnp.concatenate([g, g]).reshape(1, LANE)
    return pl.pallas_call(
        functools.partial(_mla_post_kernel, heads=heads),
        grid=(m // bm,),
        in_specs=[pl.BlockSpec((bm, q_raw.shape[1]), lambda i: (i, 0)),
                  pl.BlockSpec((bm, kv_raw.shape[1]), lambda i: (i, 0)),
                  pl.BlockSpec((bm, LANE), lambda i: (i, kr_block)),
                  pl.BlockSpec((1, QK_NOPE), lambda i: (0, 0)),
                  pl.BlockSpec((1, LANE), lambda i: (0, 0)),
                  pl.BlockSpec((1, QK_NOPE), lambda i: (0, 0)),
                  pl.BlockSpec((1, LANE), lambda i: (0, 0)),
                  pl.BlockSpec((bm, LANE), tab),
                  pl.BlockSpec((bm, LANE), tab)],
        out_specs=[pl.BlockSpec((heads, bm, B_QK_PAD), lambda i: (0, i, 0)),
                   pl.BlockSpec((heads, bm, B_QK_PAD), lambda i: (0, i, 0)),
                   pl.BlockSpec((heads, bm, HEAD_DIM), lambda i: (0, i, 0))],
        out_shape=[jax.ShapeDtypeStruct((heads, m, B_QK_PAD), BF16),
                   jax.ShapeDtypeStruct((heads, m, B_QK_PAD), BF16),
                   jax.ShapeDtypeStruct((heads, m, HEAD_DIM), BF16)],
        compiler_params=_cparams(("parallel",)),
        name="mla_post",
    )(q_raw, kv_raw, pb, qn_gain[:QK_NOPE].reshape(1, QK_NOPE), pair(qn_gain[QK_NOPE:]),
      kn_gain[:QK_NOPE].reshape(1, QK_NOPE), pair(kn_gain[QK_NOPE:]), cos_t, sin_t)


def _mla_attn_kernel(q_ref, kc_ref, vc_ref, *rest, with_lat):
    if with_lat:
        kl_ref, vl_ref, o_ref = rest
    else:
        (o_ref,) = rest
    scale = B_QK ** -0.5
    q = q_ref[...]
    s_c = _dot_nt(q, kc_ref[...]) * scale
    m = jnp.max(s_c, axis=-1, keepdims=True)
    if with_lat:
        s_l = _dot_nt(q, kl_ref[...]) * scale
        m = jnp.maximum(m, jnp.max(s_l, axis=-1, keepdims=True))
    p_c = jnp.exp(s_c - m)
    den = jnp.sum(p_c, axis=-1, keepdims=True)
    acc = jnp.dot(p_c.astype(BF16), vc_ref[...], preferred_element_type=F32)
    if with_lat:
        p_l = jnp.exp(s_l - m)
        den = den + jnp.sum(p_l, axis=-1, keepdims=True)
        acc = acc + jnp.dot(p_l.astype(BF16), vl_ref[...], preferred_element_type=F32)
    o_ref[...] = (acc / den).astype(o_ref.dtype)


def _mla_attention(qh, kh, vh, rows, heads, latent_queries):
    b, s, c = rows.batch, rows.seq, rows.ctx_len
    ctx_blk0 = rows.m_lat // c
    if latent_queries:
        tq = _pick(s, 512, 8)
        nq = s // tq
        q_map = lambda bi, h, i: (h, bi * nq + i, 0)
        o_map = lambda bi, h, i: (bi * nq + i, h)
        m_out = rows.m_lat
    else:
        tq, nq = c, 1
        q_map = lambda bi, h, i: (h, ctx_blk0 + bi, 0)
        o_map = lambda bi, h, i: (bi, h)
        m_out = rows.m_ctx
    in_specs = [pl.BlockSpec((None, tq, B_QK_PAD), q_map),
                pl.BlockSpec((None, c, B_QK_PAD), lambda bi, h, i: (h, ctx_blk0 + bi, 0)),
                pl.BlockSpec((None, c, HEAD_DIM), lambda bi, h, i: (h, ctx_blk0 + bi, 0))]
    args = [qh, kh, vh]
    if latent_queries:
        in_specs += [pl.BlockSpec((None, s, B_QK_PAD), lambda bi, h, i: (h, bi, 0)),
                     pl.BlockSpec((None, s, HEAD_DIM), lambda bi, h, i: (h, bi, 0))]
        args += [kh, vh]
    return pl.pallas_call(
        functools.partial(_mla_attn_kernel, with_lat=latent_queries),
        grid=(b, heads, nq),
        in_specs=in_specs,
        out_specs=pl.BlockSpec((tq, HEAD_DIM), o_map),
        out_shape=jax.ShapeDtypeStruct((m_out, heads * HEAD_DIM), BF16),
        compiler_params=_cparams(("parallel", "parallel", "parallel")),
        name="mla_attention",
    )(*args)


def _gqa_post_kernel(p_ref, qg_ref, kg_ref, cos_ref, sin_ref, q_ref, k_ref, v_ref, *, q_heads, kv_heads):
    cos, sin = cos_ref[...], sin_ref[...]
    half = HEAD_DIM // 4

    def norm_rope(x, gain):
        xn = x * lax.rsqrt(jnp.mean(x * x, axis=-1, keepdims=True) + EPS) * gain
        return _rope(xn, cos, sin, half)

    for h in range(q_heads):
        sl = slice(h * HEAD_DIM, (h + 1) * HEAD_DIM)
        q_ref[:, sl] = norm_rope(p_ref[:, sl], qg_ref[...]).astype(q_ref.dtype)
    for h in range(kv_heads):
        sl = slice(h * HEAD_DIM, (h + 1) * HEAD_DIM)
        k0 = q_heads * HEAD_DIM
        v0 = (q_heads + kv_heads) * HEAD_DIM
        k_ref[:, sl] = norm_rope(p_ref[:, k0 + h * HEAD_DIM:k0 + (h + 1) * HEAD_DIM],
                                 kg_ref[...]).astype(k_ref.dtype)
        v_ref[:, sl] = p_ref[:, v0 + h * HEAD_DIM:v0 + (h + 1) * HEAD_DIM].astype(v_ref.dtype)


def _gqa_post(pc, qn_gain, kn_gain, cos_t, sin_t, rows, q_heads, kv_heads):
    m, n = pc.shape
    bm = rows.block(256)
    n_lat, n_tab = rows.m_lat // bm, rows.seq // bm
    tab = lambda i: (jnp.where(i < n_lat, i % n_tab, n_tab), 0)
    return pl.pallas_call(
        functools.partial(_gqa_post_kernel, q_heads=q_heads, kv_heads=kv_heads),
        grid=(m // bm,),
        in_specs=[pl.BlockSpec((bm, n), lambda i: (i, 0)),
                  pl.BlockSpec((1, HEAD_DIM), lambda i: (0, 0)),
                  pl.BlockSpec((1, HEAD_DIM), lambda i: (0, 0)),
                  pl.BlockSpec((bm, LANE), tab),
                  pl.BlockSpec((bm, LANE), tab)],
        out_specs=[pl.BlockSpec((bm, q_heads * HEAD_DIM), lambda i: (i, 0)),
                   pl.BlockSpec((bm, kv_heads * HEAD_DIM), lambda i: (i, 0)),
                   pl.BlockSpec((bm, kv_heads * HEAD_DIM), lambda i: (i, 0))],
        out_shape=[jax.ShapeDtypeStruct((m, q_heads * HEAD_DIM), BF16),
                   jax.ShapeDtypeStruct((m, kv_heads * HEAD_DIM), BF16),
                   jax.ShapeDtypeStruct((m, kv_heads * HEAD_DIM), BF16)],
        compiler_params=_cparams(("parallel",)),
        name="gqa_post",
    )(pc, qn_gain.reshape(1, HEAD_DIM), kn_gain.reshape(1, HEAD_DIM), cos_t, sin_t)


def _gqa_attn_kernel(sink_ref, q_ref, kc_ref, vc_ref, *rest, group, seq, banded):
    if banded:
        kp_ref, kq_ref, kn_ref, vp_ref, vq_ref, vn_ref, o_ref = rest
    else:
        (o_ref,) = rest
    scale = HEAD_DIM ** -0.5
    kvh = pl.program_id(1)
    blk = pl.program_id(2)
    rows_q = group * WINDOW
    q = jnp.concatenate([q_ref[:, g * HEAD_DIM:(g + 1) * HEAD_DIM] for g in range(group)], axis=0)

    row = lax.broadcasted_iota(jnp.int32, (rows_q, 1), 0)
    sink = jnp.zeros((rows_q, 1), F32)
    for g in range(group):
        in_g = (row >= g * WINDOW) & (row < (g + 1) * WINDOW)
        sink = jnp.where(in_g, sink_ref[kvh * group + g], sink)

    s_c = _dot_nt(q, kc_ref[...]) * scale
    m = jnp.maximum(jnp.max(s_c, axis=-1, keepdims=True), sink)
    if banded:
        kb = jnp.concatenate([kp_ref[...], kq_ref[...], kn_ref[...]], axis=0)
        vb = jnp.concatenate([vp_ref[...], vq_ref[...], vn_ref[...]], axis=0)
        s_b = _dot_nt(q, kb) * scale
        qpos = lax.broadcasted_iota(jnp.int32, s_b.shape, 0) & (WINDOW - 1)
        krel = lax.broadcasted_iota(jnp.int32, s_b.shape, 1) - WINDOW
        kpos = blk * WINDOW + krel
        valid = (jnp.abs(qpos - krel) <= WINDOW) & (kpos >= 0) & (kpos < seq)
        s_b = jnp.where(valid, s_b, NEG)
        m = jnp.maximum(m, jnp.max(s_b, axis=-1, keepdims=True))
    p_c = jnp.exp(s_c - m)
    den = jnp.sum(p_c, axis=-1, keepdims=True) + jnp.exp(sink - m)
    acc = jnp.dot(p_c.astype(BF16), vc_ref[...], preferred_element_type=F32)
    if banded:
        p_b = jnp.exp(s_b - m)
        den = den + jnp.sum(p_b, axis=-1, keepdims=True)
        acc = acc + jnp.dot(p_b.astype(BF16), vb, preferred_element_type=F32)
    o = acc / den
    for g in range(group):
        o_ref[:, g * HEAD_DIM:(g + 1) * HEAD_DIM] = o[g * WINDOW:(g + 1) * WINDOW].astype(o_ref.dtype)


def _gqa_attention(qc, kc, vc, sink, rows, q_heads, kv_heads, latent_queries):
    b, s, c = rows.batch, rows.seq, rows.ctx_len
    group = q_heads // kv_heads
    gw = group * HEAD_DIM
    ctx_blk0 = rows.m_lat // c
    ctx_spec = lambda: pl.BlockSpec((c, HEAD_DIM), lambda bi, k, n: (ctx_blk0 + bi, k))
    if latent_queries:
        nb = s // WINDOW
        qrow = lambda bi, n: bi * nb + n
        m_out = rows.m_lat
        orow = qrow
    else:
        nb = c // WINDOW
        qrow = lambda bi, n: rows.m_lat // WINDOW + bi * nb + n
        orow = lambda bi, n: bi * nb + n
        m_out = rows.m_ctx
    in_specs = [pl.BlockSpec(memory_space=pltpu.SMEM),
                pl.BlockSpec((WINDOW, gw), lambda bi, k, n: (qrow(bi, n), k)),
                ctx_spec(), ctx_spec()]
    args = [sink, qc, kc, vc]
    if latent_queries:
        band = [lambda bi, k, n: (bi * nb + jnp.maximum(n - 1, 0), k),
                lambda bi, k, n: (bi * nb + n, k),
                lambda bi, k, n: (bi * nb + jnp.minimum(n + 1, nb - 1), k)]
        in_specs += [pl.BlockSpec((WINDOW, HEAD_DIM), f) for f in band] * 2
        args += [kc, kc, kc, vc, vc, vc]
    return pl.pallas_call(
        functools.partial(_gqa_attn_kernel, group=group, seq=s, banded=latent_queries),
        grid=(b, kv_heads, nb),
        in_specs=in_specs,
        out_specs=pl.BlockSpec((WINDOW, gw), lambda bi, k, n: (orow(bi, n), k)),
        out_shape=jax.ShapeDtypeStruct((m_out, q_heads * HEAD_DIM), BF16),
        compiler_params=_cparams(("parallel", "parallel", "parallel")),
        name="gqa_attention",
    )(*args)


def _extract_topk(s, pos, ids, count, val_ref, id_ref):
    beyond = 3.0e38
    for r in range(count):
        top = jnp.max(s, axis=0, keepdims=True)
        first = jnp.min(jnp.where(s == top, pos, beyond), axis=0, keepdims=True)
        hit = pos == first
        val_ref[r:r + 1, :] = top
        if ids is None:
            id_ref[r:r + 1, :] = first
        else:
            id_ref[r:r + 1, :] = jnp.max(jnp.where(hit, ids, -1.0), axis=0, keepdims=True)
        s = jnp.where(hit, -jnp.inf, s)


_PAIR_SPANS = [(a, PEER_TOPK // (a + 1)) for a in range(PEER_TOPK)]
_N_PAIRS = sum(n for _, n in _PAIR_SPANS)
_N_PAIRS_PAD = -(-_N_PAIRS // 8) * 8


def _route_kernel(q_ref, sk_ref, idx_ref, gate_ref, tv_ref, ti_ref, cs_ref, cp_ref, ci_ref, bs_ref,
                  bi_ref, *, n_keys):
    tt = q_ref.shape[0]
    key_id = lax.broadcasted_iota(jnp.int32, (n_keys, tt), 0).astype(F32)
    for p in range(2):
        qp = q_ref[:, p * LANE:(p + 1) * LANE].astype(BF16)
        scores = _dot_nt(sk_ref[p], qp)
        _extract_topk(scores, key_id, None, PEER_TOPK, tv_ref.at[p], ti_ref.at[p])
    cs_ref[...] = jnp.full(cs_ref.shape, -jnp.inf, F32)
    cp_ref[...] = jnp.full(cp_ref.shape, 1.0e6, F32)
    ci_ref[...] = jnp.full(ci_ref.shape, -1.0, F32)
    off = 0
    for a, n in _PAIR_SPANS:
        rs = slice(off, off + n)
        cs_ref[rs, :] = tv_ref[0, a:a + 1, :] + tv_ref[1, 0:n, :]
        cp_ref[rs, :] = float(a * PEER_TOPK) + lax.broadcasted_iota(jnp.int32, (n, tt), 0).astype(F32)
        ci_ref[rs, :] = ti_ref[0, a:a + 1, :] * float(n_keys) + ti_ref[1, 0:n, :]
        off += n
    _extract_topk(cs_ref[...], cp_ref[...], ci_ref[...], PEER_TOPK, bs_ref, bi_ref)
    best = bs_ref[...]
    e = jnp.exp(best - best[0:1, :])
    gate_ref[...] = e / jnp.sum(e, axis=0, keepdims=True)
    idx_ref[...] = bi_ref[...].astype(jnp.int32)


def _peer_route(pq, subkeys):
    m = pq.shape[0]
    heads, _, n_keys, half = subkeys.shape
    assert half == LANE
    tt = _pick(m, 256)
    k = PEER_TOPK
    return pl.pallas_call(
        functools.partial(_route_kernel, n_keys=n_keys),
        grid=(m // tt, heads),
        in_specs=[pl.BlockSpec((tt, 2 * LANE), lambda i, h: (i, h)),
                  pl.BlockSpec((None, 2, n_keys, half), lambda i, h: (h, 0, 0, 0))],
        out_specs=[pl.BlockSpec((None, k, tt), lambda i, h: (h, 0, i)),
                   pl.BlockSpec((None, k, tt), lambda i, h: (h, 0, i))],
        out_shape=[jax.ShapeDtypeStruct((heads, k, m), jnp.int32),
                   jax.ShapeDtypeStruct((heads, k, m), F32)],
        scratch_shapes=[pltpu.VMEM((2, k, tt), F32), pltpu.VMEM((2, k, tt), F32)]
        + [pltpu.VMEM((_N_PAIRS_PAD, tt), F32)] * 3
        + [pltpu.VMEM((k, tt), F32), pltpu.VMEM((k, tt), F32)],
        compiler_params=_cparams(("parallel", "parallel")),
        name="peer_route",
    )(pq, subkeys.astype(BF16))


HI16 = 0xFFFF0000


def _pack_kernel(u_ref, v_ref, o_ref):
    ub = lax.bitcast_convert_type(u_ref[...].astype(BF16).astype(F32), jnp.uint32)
    vb = lax.bitcast_convert_type(v_ref[...].astype(BF16).astype(F32), jnp.uint32)
    o_ref[...] = (vb & jnp.uint32(HI16)) | (ub >> 16)


def _pack_tables(u_all, v_all, layer):
    _, e, d = u_all.shape
    be = _pick(e, 256, 8)
    spec = pl.BlockSpec((None, be, d), lambda i: (layer, i, 0))
    return pl.pallas_call(
        _pack_kernel,
        grid=(e // be,),
        in_specs=[spec, spec],
        out_specs=pl.BlockSpec((be, d), lambda i: (i, 0)),
        out_shape=jax.ShapeDtypeStruct((e, d), jnp.uint32),
        compiler_params=_cparams(("parallel",)),
        name="peer_pack",
    )(u_all, v_all)


PEER_RING = 8
PEER_AHEAD = PEER_RING - 1


def _peer_kernel(idx_ref, g_ref, gate_ref, tab_hbm, o_ref, *scratch, tokens, n_sel):
    bufs, sem = scratch[:PEER_RING], scratch[PEER_RING]
    step = pl.program_id(0)
    tile = 8

    def row_copy(e, s, k):
        return pltpu.make_async_copy(tab_hbm.at[pl.ds(e, 1), :], bufs[s].at[pl.ds(k, 1), :], sem.at[s])

    def wait(s):
        pltpu.make_async_copy(tab_hbm.at[pl.ds(0, n_sel), :], bufs[s], sem.at[s]).wait()

    @pl.when(step == 0)
    def _():
        for t in range(PEER_AHEAD):
            for k in range(n_sel):
                row_copy(idx_ref[t, k], t, k).start()

    eye = (lax.broadcasted_iota(jnp.int32, (n_sel, n_sel), 0)
           == lax.broadcasted_iota(jnp.int32, (n_sel, n_sel), 1))

    def process(t, s):
        cur, s_nxt = bufs[s], (s + PEER_AHEAD) % PEER_RING
        wait(s)
        g = g_ref[pl.ds(t, 1), :]
        parts = []
        for c in range(n_sel // tile):
            for k in range(c * tile, (c + 1) * tile):
                row_copy(idx_ref[t + PEER_AHEAD, k], s_nxt, k).start(priority=k % 2)
            u = lax.bitcast_convert_type(cur[c * tile:(c + 1) * tile, :] << 16, F32)
            parts.append(jnp.sum(u * g, axis=1, keepdims=True))
        d = jnp.concatenate(parts, axis=0)
        gate_row = gate_ref[pl.ds(t, 1), :]
        gate_col = jnp.sum(jnp.where(eye, gate_row, 0.0), axis=1, keepdims=True)
        w = gate_col * _gelu(d)
        acc = jnp.zeros((tile, g.shape[1]), F32)
        for c in range(n_sel // tile):
            v = lax.bitcast_convert_type(cur[c * tile:(c + 1) * tile, :] & jnp.uint32(HI16), F32)
            acc = acc + v * w[c * tile:(c + 1) * tile, :]
        o_ref[pl.ds(t, 1), :] = jnp.sum(acc, axis=0, keepdims=True)

    def ring(i, carry):
        for s in range(PEER_RING):
            process(PEER_RING * i + s, s)
        return carry

    lax.fori_loop(0, tokens // PEER_RING, ring, 0)

    @pl.when(step == pl.num_programs(0) - 1)
    def _():
        for s in range(PEER_AHEAD):
            wait(s)


def _peer_experts(idx, gate, g, tab):
    m, d = g.shape
    n_sel = idx.shape[1]
    tb = _pick(m, 128, 8)
    nb = m // tb
    assert tb % PEER_RING == 0 and PEER_AHEAD <= 8
    blocks = idx.reshape(nb, tb, n_sel)
    idx_ext = jnp.concatenate([blocks, jnp.roll(blocks[:, :8], -1, axis=0)], axis=1)
    return pl.pallas_call(
        functools.partial(_peer_kernel, tokens=tb, n_sel=n_sel),
        grid=(nb,),
        in_specs=[pl.BlockSpec((None, tb + 8, n_sel), lambda i: (i, 0, 0), memory_space=pltpu.SMEM),
                  pl.BlockSpec((tb, d), lambda i: (i, 0)),
                  pl.BlockSpec((tb, n_sel), lambda i: (i, 0)),
                  pl.BlockSpec(memory_space=pl.ANY)],
        out_specs=pl.BlockSpec((tb, d), lambda i: (i, 0)),
        out_shape=jax.ShapeDtypeStruct((m, d), F32),
        scratch_shapes=[pltpu.VMEM((n_sel, d), jnp.uint32)] * PEER_RING
        + [pltpu.SemaphoreType.DMA((PEER_RING,))],
        compiler_params=_cparams(("arbitrary",)),
        name="peer_experts",
    )(idx_ext, g, gate, tab)


def _rope_tables(seq, rot_dim, pad):
    rows = seq // GRID_W
    row = jnp.repeat(jnp.arange(rows, dtype=F32), GRID_W)
    col = jnp.tile(jnp.arange(GRID_W, dtype=F32), rows)
    n_freq = rot_dim // 4
    inv = ROPE_BASE ** (-jnp.arange(n_freq, dtype=F32) / n_freq)
    ar, ac = row[:, None] * inv, col[:, None] * inv
    ang = jnp.concatenate([ar, ar, ac, ac], axis=-1)
    sign = jnp.concatenate([-jnp.ones(n_freq), jnp.ones(n_freq)] * 2).astype(F32)
    cos, sin = jnp.cos(ang), jnp.sin(ang) * sign
    reps = LANE // rot_dim
    cos, sin = jnp.tile(cos, (1, reps)), jnp.tile(sin, (1, reps))
    cos = jnp.concatenate([cos, jnp.ones((pad, LANE), F32)], axis=0)
    sin = jnp.concatenate([sin, jnp.zeros((pad, LANE), F32)], axis=0)
    return cos, sin


def kernel(x, c, ctx, c_ctx, w_mod, b_mod, norm1_gain, norm2_gain, w_in, a_v_gain, a_w_s, a_b_s,
           b_q_gain, b_kv_gain, b_w_uq, b_w_ukv, b_qn_gain, b_kn_gain, c_qn_gain, c_kn_gain, c_sink,
           w_out, peer_w_q, peer_subkeys, peer_u, peer_v):
    batch, seq, d = x.shape
    ctx_len = ctx.shape[1]
    depth = w_mod.shape[0]
    a_width = a_v_gain.shape[1]
    q_lora, kv_lora = b_q_gain.shape[1], b_kv_gain.shape[1]
    b_heads = b_w_uq.shape[2] // B_QK
    c_heads = c_sink.shape[1]
    c_kv = c_heads // 3
    p_a = 2 * a_width
    p_b = q_lora + kv_lora + QK_ROPE
    p_b_pad = -(-p_b // (2 * LANE)) * (2 * LANE)
    assert b_heads % 2 == 0 and (q_lora + kv_lora) % LANE == 0
    assert seq % WINDOW == 0 and ctx_len % WINDOW == 0

    rows_all = _Rows(batch, seq, ctx_len)
    bm_tab = rows_all.block(256)
    cos_b, sin_b = _rope_tables(seq, QK_ROPE, bm_tab)
    cos_c, sin_c = _rope_tables(seq, HEAD_DIM, bm_tab)

    c8 = jnp.zeros((8, d), F32).at[:batch].set(c).at[batch].set(c_ctx)
    xs = jnp.concatenate([x.reshape(batch * seq, d), ctx.reshape(batch * ctx_len, d)], axis=0)

    for layer in range(depth):
        need_ctx = layer < depth - 1
        mod3 = _modulation(c8, w_mod, b_mod[layer], layer).reshape(8, 1, N_MOD * d)

        w_l = w_in[layer]
        w_a = w_l[:, :p_a].astype(BF16)
        w_b = jnp.pad(w_l[:, p_a:p_a + p_b], ((0, 0), (0, p_b_pad - p_b))).astype(BF16)
        w_c = w_l[:, p_a + p_b:].astype(BF16)
        w_uq = b_w_uq[layer].reshape(q_lora, b_heads, B_QK)
        w_uq = jnp.concatenate([w_uq[:, :, :QK_NOPE].reshape(q_lora, -1),
                                w_uq[:, :, QK_NOPE:].reshape(q_lora, -1)], axis=1).astype(BF16)
        w_ukv = b_w_ukv[layer].reshape(kv_lora, b_heads, QK_NOPE + HEAD_DIM)
        w_ukv = jnp.concatenate([w_ukv[:, :, :QK_NOPE].reshape(kv_lora, -1),
                                 w_ukv[:, :, QK_NOPE:].reshape(kv_lora, -1)], axis=1).astype(BF16)

        (h,) = _norm_mod(xs, norm1_gain[layer], mod3, 0, 1, rows_all, [BF16])
        pa = _matmul(h, w_a)
        pb = _matmul(h, w_b)
        pc = _matmul(h, w_c)

        oa = _gmlp(pa, a_v_gain[layer], a_w_s[layer], a_b_s[layer])

        cq, ckv = _mla_pre(pb, b_q_gain[layer], b_kv_gain[layer], rows_all)
        q_raw = _matmul(cq, w_uq)
        kv_raw = _matmul(ckv, w_ukv)
        qh, kh, vh = _mla_post(q_raw, kv_raw, pb, (q_lora + kv_lora) // LANE, b_qn_gain[layer],
                               b_kn_gain[layer], cos_b, sin_b, rows_all, b_heads)
        ob = _mla_attention(qh, kh, vh, rows_all, b_heads, True)

        qc, kc, vc = _gqa_post(pc, c_qn_gain[layer], c_kn_gain[layer], cos_c, sin_c, rows_all,
                               c_heads, c_kv)
        oc = _gqa_attention(qc, kc, vc, c_sink[layer], rows_all, c_heads, c_kv, True)

        if need_ctx:
            rows = rows_all
            ob = jnp.concatenate([ob, _mla_attention(qh, kh, vh, rows_all, b_heads, False)], axis=0)
            oc = jnp.concatenate(
                [oc, _gqa_attention(qc, kc, vc, c_sink[layer], rows_all, c_heads, c_kv, False)], axis=0)
        else:
            rows = _Rows(batch, seq, 0)
            oa = oa[:rows.m_lat]
            xs = xs[:rows.m_lat]
        o = jnp.concatenate([oa, ob, oc], axis=1)
        xs = _matmul_residual(o, w_out[layer].astype(BF16), xs, mod3, 2, rows)

        g_bf, g = _norm_mod(xs, norm2_gain[layer], mod3, 3, 4, rows, [BF16, F32])
        pq = _matmul(g_bf, peer_w_q[layer].astype(BF16))
        idx, gate = _peer_route(pq, peer_subkeys[layer])
        n_sel = idx.shape[0] * idx.shape[1]
        idx = idx.reshape(n_sel, -1).T
        gate = gate.reshape(n_sel, -1).T
        y = _peer_experts(idx, gate, g, _pack_tables(peer_u, peer_v, layer))
        xs = _gated_add(xs, y, mod3, 5, rows)

    return xs[:batch * seq].reshape(batch, seq, d)
```

```python
import functools

import jax
import jax.numpy as jnp
from jax import lax
from jax.experimental import pallas as pl
from jax.experimental.pallas import tpu as pltpu

F32 = jnp.float32
BF16 = jnp.bfloat16

LANE = 128
HEAD_DIM = 128
QK_NOPE = 128
QK_ROPE = 64
B_QK = QK_NOPE + QK_ROPE
B_QK_PAD = 2 * LANE
GRID_W = 64
WINDOW = 128
CHUNK = 128
PEER_TOPK = 16
N_MOD = 6
ROPE_BASE = 10000.0
EPS = 1e-6
NEG = -1e30
LOG2E = 1.4426950408889634
MIB = 1024 * 1024
VMEM_LIMIT = 56 * MIB


def _cparams(semantics, vmem=VMEM_LIMIT):
    return pltpu.CompilerParams(dimension_semantics=semantics, vmem_limit_bytes=vmem)


def _pick(n, target, align=LANE):
    if n <= target:
        return n
    best = None
    d = align
    while d <= target:
        if n % d == 0:
            best = d
        d += align
    assert best is not None, (n, target, align)
    return best


def _gelu(x):
    return 0.5 * x * (1.0 + jnp.tanh(0.7978845608028654 * (x + 0.044715 * (x * x * x))))


def _dot_nt(a, b):
    return lax.dot_general(a, b, (((1,), (1,)), ((), ())), preferred_element_type=F32)


def _modulation_kernel(c_ref, w_ref, b_ref, o_ref, acc_ref):
    k = pl.program_id(1)

    @pl.when(k == 0)
    def _():
        acc_ref[...] = jnp.zeros_like(acc_ref)

    c = c_ref[...]
    silu = c / (1.0 + jnp.exp(-c))
    acc_ref[...] += jnp.dot(silu.astype(BF16), w_ref[...].astype(BF16), preferred_element_type=F32)

    @pl.when(k == pl.num_programs(1) - 1)
    def _():
        o_ref[...] = acc_ref[...] + b_ref[...]


def _modulation(c8, w_all, b, layer):
    rows, d = c8.shape
    n = w_all.shape[2]
    bn = _pick(n, 2048)
    bk = _pick(d, 1024)
    return pl.pallas_call(
        _modulation_kernel,
        grid=(n // bn, d // bk),
        in_specs=[pl.BlockSpec((rows, bk), lambda j, k: (0, k)),
                  pl.BlockSpec((None, bk, bn), lambda j, k: (layer, k, j)),
                  pl.BlockSpec((1, bn), lambda j, k: (0, j))],
        out_specs=pl.BlockSpec((rows, bn), lambda j, k: (0, j)),
        out_shape=jax.ShapeDtypeStruct((rows, n), F32),
        scratch_shapes=[pltpu.VMEM((rows, bn), F32)],
        compiler_params=_cparams(("parallel", "arbitrary")),
        name="modulation",
    )(c8, w_all, b.reshape(1, n))


class _Rows:
    def __init__(self, batch, seq, ctx_len):
        self.batch, self.seq, self.ctx_len = batch, seq, ctx_len
        self.m_lat = batch * seq
        self.m_ctx = batch * ctx_len
        self.m = self.m_lat + self.m_ctx

    def block(self, target):
        bm = target
        while self.seq % bm or self.m_ctx % bm:
            bm //= 2
        assert bm >= 8
        return bm

    def mod_row(self, i, bm):
        return jnp.where(i < self.m_lat // bm, (i * bm) // self.seq, self.batch)


def _norm_mod_kernel(x_ref, gain_ref, shift_ref, scale_ref, *out_refs):
    x = x_ref[...]
    y = x * lax.rsqrt(jnp.mean(x * x, axis=-1, keepdims=True) + EPS) * gain_ref[...]
    h = y * (1.0 + scale_ref[0]) + shift_ref[0]
    for o in out_refs:
        o[...] = h.astype(o.dtype)


def _norm_mod(x, gain, mod3, shift_chunk, scale_chunk, rows, out_dtypes):
    m, d = x.shape
    bm = rows.block(256)
    outs = pl.pallas_call(
        _norm_mod_kernel,
        grid=(m // bm,),
        in_specs=[pl.BlockSpec((bm, d), lambda i: (i, 0)),
                  pl.BlockSpec((1, d), lambda i: (0, 0)),
                  pl.BlockSpec((1, 1, d), lambda i: (rows.mod_row(i, bm), 0, shift_chunk)),
                  pl.BlockSpec((1, 1, d), lambda i: (rows.mod_row(i, bm), 0, scale_chunk))],
        out_specs=[pl.BlockSpec((bm, d), lambda i: (i, 0)) for _ in out_dtypes],
        out_shape=[jax.ShapeDtypeStruct((m, d), dt) for dt in out_dtypes],
        compiler_params=_cparams(("parallel",)),
        name="norm_mod",
    )(x, gain.reshape(1, d), mod3, mod3)
    return outs


def _mm_kernel(a_ref, w_ref, o_ref):
    o_ref[...] = jnp.dot(a_ref[...], w_ref[...], preferred_element_type=F32).astype(o_ref.dtype)


def _mm_res_kernel(*refs, parts):
    a_refs, w_refs = refs[:parts], refs[parts:2 * parts]
    res_ref, gate_ref, o_ref = refs[2 * parts:]
    acc = jnp.dot(a_refs[0][...], w_refs[0][...], preferred_element_type=F32)
    for a_ref, w_ref in zip(a_refs[1:], w_refs[1:]):
        acc = acc + jnp.dot(a_ref[...], w_ref[...], preferred_element_type=F32)
    o_ref[...] = res_ref[...] + gate_ref[0] * acc


def _matmul(a, w, out_dtype=F32, bm_target=1024, bn_target=1024):
    m, k = a.shape
    n = w.shape[1]
    bm = _pick(m, bm_target, 8)
    bn = _pick(n, bn_target)
    return pl.pallas_call(
        _mm_kernel,
        grid=(m // bm, n // bn),
        in_specs=[pl.BlockSpec((bm, k), lambda i, j: (i, 0)),
                  pl.BlockSpec((k, bn), lambda i, j: (0, j))],
        out_specs=pl.BlockSpec((bm, bn), lambda i, j: (i, j)),
        out_shape=jax.ShapeDtypeStruct((m, n), out_dtype),
        compiler_params=_cparams(("parallel", "parallel")),
        name="matmul",
    )(a, w)


def _matmul_residual(a_parts, w_parts, res, mod3, gate_chunk, rows):
    m, n = res.shape
    bm = rows.block(1024)
    bn = _pick(n, 512)
    parts = len(a_parts)
    a_specs = [pl.BlockSpec((bm, a.shape[1]), lambda i, j: (i, 0)) for a in a_parts]
    w_specs = [pl.BlockSpec((w.shape[0], bn), lambda i, j: (0, j)) for w in w_parts]
    return pl.pallas_call(
        functools.partial(_mm_res_kernel, parts=parts),
        grid=(m // bm, n // bn),
        in_specs=a_specs + w_specs + [
            pl.BlockSpec((bm, bn), lambda i, j: (i, j)),
            pl.BlockSpec((1, 1, bn), lambda i, j: (rows.mod_row(i, bm), 0, gate_chunk * (n // bn) + j))],
        out_specs=pl.BlockSpec((bm, bn), lambda i, j: (i, j)),
        out_shape=jax.ShapeDtypeStruct((m, n), F32),
        compiler_params=_cparams(("parallel", "parallel")),
        name="matmul_residual",
    )(*a_parts, *w_parts, res, mod3)


def _gmlp_kernel(p_ref, gain_ref, ws_ref, bs_ref, o_ref, *, width, chunks):
    ws = ws_ref[...].astype(BF16)
    bs = bs_ref[...]
    for c in range(chunks):
        r0 = c * CHUNK
        z = _gelu(p_ref[r0:r0 + CHUNK, :])
        u = z[:, :width]
        parts = []
        for h in range(width // HEAD_DIM):
            vh = z[:, width + h * HEAD_DIM: width + (h + 1) * HEAD_DIM]
            vh = vh * lax.rsqrt(jnp.mean(vh * vh, axis=-1, keepdims=True) + EPS)
            parts.append((vh * gain_ref[:, h * HEAD_DIM:(h + 1) * HEAD_DIM]).astype(BF16))
        vn = jnp.concatenate(parts, axis=1)
        s = jnp.dot(ws, vn, preferred_element_type=F32) + bs
        o_ref[r0:r0 + CHUNK, :] = (u * s).astype(o_ref.dtype)


def _gmlp(pa, v_gain, w_s, b_s):
    m, two_w = pa.shape
    width = two_w // 2
    chunks = 2
    bm = chunks * CHUNK
    return pl.pallas_call(
        functools.partial(_gmlp_kernel, width=width, chunks=chunks),
        grid=(m // bm,),
        in_specs=[pl.BlockSpec((bm, two_w), lambda i: (i, 0)),
                  pl.BlockSpec((1, width), lambda i: (0, 0)),
                  pl.BlockSpec((CHUNK, CHUNK), lambda i: (0, 0)),
                  pl.BlockSpec((CHUNK, 1), lambda i: (0, 0))],
        out_specs=pl.BlockSpec((bm, width), lambda i: (i, 0)),
        out_shape=jax.ShapeDtypeStruct((m, width), BF16),
        compiler_params=_cparams(("parallel",)),
        name="gmlp",
    )(pa, v_gain.reshape(1, width), w_s, b_s.reshape(CHUNK, 1))


def _rope(x, cos, sin_signed, half):
    lane = lax.broadcasted_iota(jnp.int32, x.shape, 1)
    first = (lane & (2 * half - 1)) < half
    partner = jnp.where(first, pltpu.roll(x, LANE - half, axis=1), pltpu.roll(x, half, axis=1))
    return x * cos + partner * sin_signed


def _mla_pre_kernel(p_ref, qg_ref, kvg_ref, cq_ref, ckv_ref, *, q_lora, kv_lora):
    cq = p_ref[:, :q_lora]
    cq_ref[...] = (cq * lax.rsqrt(jnp.mean(cq * cq, axis=-1, keepdims=True) + EPS)
                   * qg_ref[...]).astype(cq_ref.dtype)
    ckv = p_ref[:, q_lora:q_lora + kv_lora]
    ckv_ref[...] = (ckv * lax.rsqrt(jnp.mean(ckv * ckv, axis=-1, keepdims=True) + EPS)
                    * kvg_ref[...]).astype(ckv_ref.dtype)


def _mla_pre(pb, q_gain, kv_gain, rows):
    m, n = pb.shape
    q_lora, kv_lora = q_gain.shape[0], kv_gain.shape[0]
    bm = rows.block(256)
    return pl.pallas_call(
        functools.partial(_mla_pre_kernel, q_lora=q_lora, kv_lora=kv_lora),
        grid=(m // bm,),
        in_specs=[pl.BlockSpec((bm, n), lambda i: (i, 0)),
                  pl.BlockSpec((1, q_lora), lambda i: (0, 0)),
                  pl.BlockSpec((1, kv_lora), lambda i: (0, 0))],
        out_specs=[pl.BlockSpec((bm, q_lora), lambda i: (i, 0)),
                   pl.BlockSpec((bm, kv_lora), lambda i: (i, 0))],
        out_shape=[jax.ShapeDtypeStruct((m, q_lora), BF16),
                   jax.ShapeDtypeStruct((m, kv_lora), BF16)],
        compiler_params=_cparams(("parallel",)),
        name="mla_pre",
    )(pb, q_gain.reshape(1, q_lora), kv_gain.reshape(1, kv_lora))


def _mla_post_kernel(q_ref, kv_ref, kr_ref, qgn_ref, qgr_ref, kgn_ref, kgr_ref, cos_ref, sin_ref,
                     qh_ref, kh_ref, vh_ref, *, heads):
    cos, sin = cos_ref[...], sin_ref[...]
    lane = lax.broadcasted_iota(jnp.int32, cos.shape, 1)
    low = lane < QK_ROPE
    half = QK_ROPE // 4
    zeros = jnp.zeros(cos.shape, F32)

    kr = jnp.where(low, kr_ref[...], 0.0)
    kr_ss = jnp.sum(kr * kr, axis=-1, keepdims=True)
    kr_rot = _rope(kr * kgr_ref[...], cos, sin, half)

    for hp in range(heads // 2):
        qr = q_ref[:, heads * QK_NOPE + hp * LANE: heads * QK_NOPE + (hp + 1) * LANE]
        qr2 = qr * qr
        ss_lo = jnp.sum(jnp.where(low, qr2, 0.0), axis=-1, keepdims=True)
        ss_hi = jnp.sum(jnp.where(low, 0.0, qr2), axis=-1, keepdims=True)
        rinv = []
        for j, ss_r in enumerate((ss_lo, ss_hi)):
            h = 2 * hp + j
            qn = q_ref[:, h * QK_NOPE:(h + 1) * QK_NOPE]
            r = lax.rsqrt((jnp.sum(qn * qn, axis=-1, keepdims=True) + ss_r) * (1.0 / B_QK) + EPS)
            rinv.append(r)
            qh_ref[h, :, :QK_NOPE] = (qn * r * qgn_ref[...]).astype(qh_ref.dtype)
        qrot = _rope(qr * jnp.where(low, rinv[0], rinv[1]) * qgr_ref[...], cos, sin, half)
        qh_ref[2 * hp, :, QK_NOPE:] = jnp.where(low, qrot, zeros).astype(qh_ref.dtype)
        qh_ref[2 * hp + 1, :, QK_NOPE:] = jnp.where(
            low, pltpu.roll(qrot, QK_ROPE, axis=1), zeros).astype(qh_ref.dtype)

    for h in range(heads):
        kn = kv_ref[:, h * QK_NOPE:(h + 1) * QK_NOPE]
        r = lax.rsqrt((jnp.sum(kn * kn, axis=-1, keepdims=True) + kr_ss) * (1.0 / B_QK) + EPS)
        kh_ref[h, :, :QK_NOPE] = (kn * r * kgn_ref[...]).astype(kh_ref.dtype)
        kh_ref[h, :, QK_NOPE:] = (kr_rot * r).astype(kh_ref.dtype)
        vh_ref[h] = kv_ref[:, (heads + h) * HEAD_DIM:(heads + h + 1) * HEAD_DIM].astype(vh_ref.dtype)


def _mla_post(q_raw, kv_raw, pb, kr_block, qn_gain, kn_gain, cos_t, sin_t, rows, heads):
    m = q_raw.shape[0]
    bm = rows.block(256)
    n_lat, n_tab = rows.m_lat // bm, rows.seq // bm
    tab = lambda i: (jnp.where(i < n_lat, i % n_tab, n_tab), 0)
    pair = lambda g: jnp.concatenate([g, g]).reshape(1, LANE)
    return pl.pallas_call(
        functools.partial(_mla_post_kernel, heads=heads),
        grid=(m // bm,),
        in_specs=[pl.BlockSpec((bm, q_raw.shape[1]), lambda i: (i, 0)),
                  pl.BlockSpec((bm, kv_raw.shape[1]), lambda i: (i, 0)),
                  pl.BlockSpec((bm, LANE), lambda i: (i, kr_block)),
                  pl.BlockSpec((1, QK_NOPE), lambda i: (0, 0)),
                  pl.BlockSpec((1, LANE), lambda i: (0, 0)),
                  pl.BlockSpec((1, QK_NOPE), lambda i: (0, 0)),
                  pl.BlockSpec((1, LANE), lambda i: (0, 0)),
                  pl.BlockSpec((bm, LANE), tab),
                  pl.BlockSpec((bm, LANE), tab)],
        out_specs=[pl.BlockSpec((heads, bm, B_QK_PAD), lambda i: (0, i, 0)),
                   pl.BlockSpec((heads, bm, B_QK_PAD), lambda i: (0, i, 0)),
                   pl.BlockSpec((heads, bm, HEAD_DIM), lambda i: (0, i, 0))],
        out_shape=[jax.ShapeDtypeStruct((heads, m, B_QK_PAD), BF16),
                   jax.ShapeDtypeStruct((heads, m, B_QK_PAD), BF16),
                   jax.ShapeDtypeStruct((heads, m, HEAD_DIM), BF16)],
        compiler_params=_cparams(("parallel",)),
        name="mla_post",
    )(q_raw, kv_raw, pb, qn_gain[:QK_NOPE].reshape(1, QK_NOPE), pair(qn_gain[QK_NOPE:]),
      kn_gain[:QK_NOPE].reshape(1, QK_NOPE), pair(kn_gain[QK_NOPE:]), cos_t, sin_t)


def _mla_attn_kernel(q_ref, kc_ref, vc_ref, *rest, with_lat, sub=2):
    if with_lat:
        kl_ref, vl_ref, o_ref = rest
    else:
        (o_ref,) = rest
    c = (B_QK ** -0.5) * LOG2E
    tq = q_ref.shape[0]
    rg = tq // sub if tq % sub == 0 else tq
    for r0 in range(0, tq, rg):
        q = q_ref[r0:r0 + rg, :]
        s_c = _dot_nt(q, kc_ref[...])
        m = jnp.max(s_c, axis=-1, keepdims=True)
        if with_lat:
            s_l = _dot_nt(q, kl_ref[...])
            m = jnp.maximum(m, jnp.max(s_l, axis=-1, keepdims=True))
        p_c = jnp.exp2((s_c - m) * c)
        den = jnp.sum(p_c, axis=-1, keepdims=True)
        acc = jnp.dot(p_c.astype(BF16), vc_ref[...], preferred_element_type=F32)
        if with_lat:
            p_l = jnp.exp2((s_l - m) * c)
            den = den + jnp.sum(p_l, axis=-1, keepdims=True)
            acc = acc + jnp.dot(p_l.astype(BF16), vl_ref[...], preferred_element_type=F32)
        o_ref[r0:r0 + rg, :] = (acc / den).astype(o_ref.dtype)


def _mla_attention(qh, kh, vh, rows, heads, latent_queries):
    b, s, c = rows.batch, rows.seq, rows.ctx_len
    ctx_blk0 = rows.m_lat // c
    if latent_queries:
        tq = _pick(s, 512, 8)
        nq = s // tq
        q_map = lambda bi, h, i: (h, bi * nq + i, 0)
        o_map = lambda bi, h, i: (bi * nq + i, h)
        m_out = rows.m_lat
    else:
        tq, nq = c, 1
        q_map = lambda bi, h, i: (h, ctx_blk0 + bi, 0)
        o_map = lambda bi, h, i: (bi, h)
        m_out = rows.m_ctx
    in_specs = [pl.BlockSpec((None, tq, B_QK_PAD), q_map),
                pl.BlockSpec((None, c, B_QK_PAD), lambda bi, h, i: (h, ctx_blk0 + bi, 0)),
                pl.BlockSpec((None, c, HEAD_DIM), lambda bi, h, i: (h, ctx_blk0 + bi, 0))]
    args = [qh, kh, vh]
    if latent_queries:
        in_specs += [pl.BlockSpec((None, s, B_QK_PAD), lambda bi, h, i: (h, bi, 0)),
                     pl.BlockSpec((None, s, HEAD_DIM), lambda bi, h, i: (h, bi, 0))]
        args += [kh, vh]
    return pl.pallas_call(
        functools.partial(_mla_attn_kernel, with_lat=latent_queries),
        grid=(b, heads, nq),
        in_specs=in_specs,
        out_specs=pl.BlockSpec((tq, HEAD_DIM), o_map),
        out_shape=jax.ShapeDtypeStruct((m_out, heads * HEAD_DIM), BF16),
        compiler_params=_cparams(("parallel", "parallel", "parallel")),
        name="mla_attention",
    )(*args)


def _gqa_post_kernel(p_ref, qg_ref, kg_ref, cos_ref, sin_ref, q_ref, k_ref, v_ref, *, q_heads, kv_heads):
    cos, sin = cos_ref[...], sin_ref[...]
    half = HEAD_DIM // 4

    def norm_rope(x, gain):
        xn = x * lax.rsqrt(jnp.mean(x * x, axis=-1, keepdims=True) + EPS) * gain
        return _rope(xn, cos, sin, half)

    for h in range(q_heads):
        sl = slice(h * HEAD_DIM, (h + 1) * HEAD_DIM)
        q_ref[:, sl] = norm_rope(p_ref[:, sl], qg_ref[...]).astype(q_ref.dtype)
    for h in range(kv_heads):
        sl = slice(h * HEAD_DIM, (h + 1) * HEAD_DIM)
        k0 = q_heads * HEAD_DIM
        v0 = (q_heads + kv_heads) * HEAD_DIM
        k_ref[:, sl] = norm_rope(p_ref[:, k0 + h * HEAD_DIM:k0 + (h + 1) * HEAD_DIM],
                                 kg_ref[...]).astype(k_ref.dtype)
        v_ref[:, sl] = p_ref[:, v0 + h * HEAD_DIM:v0 + (h + 1) * HEAD_DIM].astype(v_ref.dtype)


def _gqa_post(pc, qn_gain, kn_gain, cos_t, sin_t, rows, q_heads, kv_heads):
    m, n = pc.shape
    bm = rows.block(256)
    n_lat, n_tab = rows.m_lat // bm, rows.seq // bm
    tab = lambda i: (jnp.where(i < n_lat, i % n_tab, n_tab), 0)
    return pl.pallas_call(
        functools.partial(_gqa_post_kernel, q_heads=q_heads, kv_heads=kv_heads),
        grid=(m // bm,),
        in_specs=[pl.BlockSpec((bm, n), lambda i: (i, 0)),
                  pl.BlockSpec((1, HEAD_DIM), lambda i: (0, 0)),
                  pl.BlockSpec((1, HEAD_DIM), lambda i: (0, 0)),
                  pl.BlockSpec((bm, LANE), tab),
                  pl.BlockSpec((bm, LANE), tab)],
        out_specs=[pl.BlockSpec((bm, q_heads * HEAD_DIM), lambda i: (i, 0)),
                   pl.BlockSpec((bm, kv_heads * HEAD_DIM), lambda i: (i, 0)),
                   pl.BlockSpec((bm, kv_heads * HEAD_DIM), lambda i: (i, 0))],
        out_shape=[jax.ShapeDtypeStruct((m, q_heads * HEAD_DIM), BF16),
                   jax.ShapeDtypeStruct((m, kv_heads * HEAD_DIM), BF16),
                   jax.ShapeDtypeStruct((m, kv_heads * HEAD_DIM), BF16)],
        compiler_params=_cparams(("parallel",)),
        name="gqa_post",
    )(pc, qn_gain.reshape(1, HEAD_DIM), kn_gain.reshape(1, HEAD_DIM), cos_t, sin_t)


def _gqa_attn_kernel(sink_ref, q_ref, kc_ref, vc_ref, *rest, group, seq, banded):
    if banded:
        kp_ref, kq_ref, kn_ref, vp_ref, vq_ref, vn_ref, o_ref = rest
    else:
        (o_ref,) = rest
    scale = HEAD_DIM ** -0.5
    blk = pl.program_id(1)
    rows_q = group * WINDOW
    kv_heads = kc_ref.shape[1] // HEAD_DIM
    row = lax.broadcasted_iota(jnp.int32, (rows_q, 1), 0)

    for kvh in range(kv_heads):
        hs = slice(kvh * HEAD_DIM, (kvh + 1) * HEAD_DIM)
        q0 = kvh * group * HEAD_DIM
        q = jnp.concatenate([q_ref[:, q0 + g * HEAD_DIM:q0 + (g + 1) * HEAD_DIM]
                             for g in range(group)], axis=0)
        sink = jnp.zeros((rows_q, 1), F32)
        for g in range(group):
            in_g = (row >= g * WINDOW) & (row < (g + 1) * WINDOW)
            sink = jnp.where(in_g, sink_ref[kvh * group + g], sink)

        s_c = _dot_nt(q, kc_ref[:, hs]) * scale
        m = jnp.maximum(jnp.max(s_c, axis=-1, keepdims=True), sink)
        if banded:
            kb = jnp.concatenate([kp_ref[:, hs], kq_ref[:, hs], kn_ref[:, hs]], axis=0)
            vb = jnp.concatenate([vp_ref[:, hs], vq_ref[:, hs], vn_ref[:, hs]], axis=0)
            s_b = _dot_nt(q, kb) * scale
            qpos = lax.broadcasted_iota(jnp.int32, s_b.shape, 0) & (WINDOW - 1)
            krel = lax.broadcasted_iota(jnp.int32, s_b.shape, 1) - WINDOW
            kpos = blk * WINDOW + krel
            valid = (jnp.abs(qpos - krel) <= WINDOW) & (kpos >= 0) & (kpos < seq)
            s_b = jnp.where(valid, s_b, NEG)
            m = jnp.maximum(m, jnp.max(s_b, axis=-1, keepdims=True))
        p_c = jnp.exp(s_c - m)
        den = jnp.sum(p_c, axis=-1, keepdims=True) + jnp.exp(sink - m)
        acc = jnp.dot(p_c.astype(BF16), vc_ref[:, hs], preferred_element_type=F32)
        if banded:
            p_b = jnp.exp(s_b - m)
            den = den + jnp.sum(p_b, axis=-1, keepdims=True)
            acc = acc + jnp.dot(p_b.astype(BF16), vb, preferred_element_type=F32)
        o = acc / den
        for g in range(group):
            o_ref[:, q0 + g * HEAD_DIM:q0 + (g + 1) * HEAD_DIM] = (
                o[g * WINDOW:(g + 1) * WINDOW].astype(o_ref.dtype))


def _gqa_attention(qc, kc, vc, sink, rows, q_heads, kv_heads, latent_queries):
    b, s, c = rows.batch, rows.seq, rows.ctx_len
    group = q_heads // kv_heads
    qw, kw = q_heads * HEAD_DIM, kv_heads * HEAD_DIM
    ctx_blk0 = rows.m_lat // c
    ctx_spec = lambda: pl.BlockSpec((c, kw), lambda bi, n: (ctx_blk0 + bi, 0))
    if latent_queries:
        nb = s // WINDOW
        qrow = lambda bi, n: bi * nb + n
        m_out = rows.m_lat
        orow = qrow
    else:
        nb = c // WINDOW
        qrow = lambda bi, n: rows.m_lat // WINDOW + bi * nb + n
        orow = lambda bi, n: bi * nb + n
        m_out = rows.m_ctx
    in_specs = [pl.BlockSpec(memory_space=pltpu.SMEM),
                pl.BlockSpec((WINDOW, qw), lambda bi, n: (qrow(bi, n), 0)),
                ctx_spec(), ctx_spec()]
    args = [sink, qc, kc, vc]
    if latent_queries:
        band = [lambda bi, n: (bi * nb + jnp.maximum(n - 1, 0), 0),
                lambda bi, n: (bi * nb + n, 0),
                lambda bi, n: (bi * nb + jnp.minimum(n + 1, nb - 1), 0)]
        in_specs += [pl.BlockSpec((WINDOW, kw), f) for f in band] * 2
        args += [kc, kc, kc, vc, vc, vc]
    return pl.pallas_call(
        functools.partial(_gqa_attn_kernel, group=group, seq=s, banded=latent_queries),
        grid=(b, nb),
        in_specs=in_specs,
        out_specs=pl.BlockSpec((WINDOW, qw), lambda bi, n: (orow(bi, n), 0)),
        out_shape=jax.ShapeDtypeStruct((m_out, qw), BF16),
        compiler_params=_cparams(("parallel", "parallel")),
        name="gqa_attention",
    )(*args)


def _extract_topk(s, pos, ids, count, val_ref, id_ref):
    beyond = 3.0e38
    for r in range(count):
        top = jnp.max(s, axis=0, keepdims=True)
        first = jnp.min(jnp.where(s == top, pos, beyond), axis=0, keepdims=True)
        hit = pos == first
        val_ref[r:r + 1, :] = top
        if ids is None:
            id_ref[r:r + 1, :] = first
        else:
            id_ref[r:r + 1, :] = jnp.max(jnp.where(hit, ids, -1.0), axis=0, keepdims=True)
        s = jnp.where(hit, -jnp.inf, s)


_PAIR_SPANS = [(a, PEER_TOPK // (a + 1)) for a in range(PEER_TOPK)]
_N_PAIRS = sum(n for _, n in _PAIR_SPANS)
_N_PAIRS_PAD = -(-_N_PAIRS // 8) * 8


def _route_kernel(q_ref, sk_ref, idx_ref, gate_ref, tv_ref, ti_ref, cs_ref, cp_ref, ci_ref, bs_ref,
                  bi_ref, *, n_keys):
    tt = q_ref.shape[0]
    key_id = lax.broadcasted_iota(jnp.int32, (n_keys, tt), 0).astype(F32)
    for p in range(2):
        qp = q_ref[:, p * LANE:(p + 1) * LANE].astype(BF16)
        scores = _dot_nt(sk_ref[p], qp)
        _extract_topk(scores, key_id, None, PEER_TOPK, tv_ref.at[p], ti_ref.at[p])
    cs_ref[...] = jnp.full(cs_ref.shape, -jnp.inf, F32)
    cp_ref[...] = jnp.full(cp_ref.shape, 1.0e6, F32)
    ci_ref[...] = jnp.full(ci_ref.shape, -1.0, F32)
    off = 0
    for a, n in _PAIR_SPANS:
        rs = slice(off, off + n)
        cs_ref[rs, :] = tv_ref[0, a:a + 1, :] + tv_ref[1, 0:n, :]
        cp_ref[rs, :] = float(a * PEER_TOPK) + lax.broadcasted_iota(jnp.int32, (n, tt), 0).astype(F32)
        ci_ref[rs, :] = ti_ref[0, a:a + 1, :] * float(n_keys) + ti_ref[1, 0:n, :]
        off += n
    _extract_topk(cs_ref[...], cp_ref[...], ci_ref[...], PEER_TOPK, bs_ref, bi_ref)
    best = bs_ref[...]
    e = jnp.exp(best - best[0:1, :])
    gate_ref[...] = e / jnp.sum(e, axis=0, keepdims=True)
    idx_ref[...] = bi_ref[...].astype(jnp.int32)


def _peer_route(pq, subkeys):
    m = pq.shape[0]
    heads, _, n_keys, half = subkeys.shape
    assert half == LANE
    tt = _pick(m, 256)
    k = PEER_TOPK
    return pl.pallas_call(
        functools.partial(_route_kernel, n_keys=n_keys),
        grid=(m // tt, heads),
        in_specs=[pl.BlockSpec((tt, 2 * LANE), lambda i, h: (i, h)),
                  pl.BlockSpec((None, 2, n_keys, half), lambda i, h: (h, 0, 0, 0))],
        out_specs=[pl.BlockSpec((None, k, tt), lambda i, h: (h, 0, i)),
                   pl.BlockSpec((None, k, tt), lambda i, h: (h, 0, i))],
        out_shape=[jax.ShapeDtypeStruct((heads, k, m), jnp.int32),
                   jax.ShapeDtypeStruct((heads, k, m), F32)],
        scratch_shapes=[pltpu.VMEM((2, k, tt), F32), pltpu.VMEM((2, k, tt), F32)]
        + [pltpu.VMEM((_N_PAIRS_PAD, tt), F32)] * 3
        + [pltpu.VMEM((k, tt), F32), pltpu.VMEM((k, tt), F32)],
        compiler_params=_cparams(("parallel", "parallel")),
        name="peer_route",
    )(pq, subkeys.astype(BF16))


HI16 = 0xFFFF0000


def _pack_kernel(u_ref, v_ref, o_ref):
    ub = lax.bitcast_convert_type(u_ref[...].astype(BF16).astype(F32), jnp.uint32)
    vb = lax.bitcast_convert_type(v_ref[...].astype(BF16).astype(F32), jnp.uint32)
    o_ref[...] = (vb & jnp.uint32(HI16)) | (ub >> 16)


def _pack_tables(u_all, v_all, layer):
    _, e, d = u_all.shape
    be = _pick(e, 256, 8)
    spec = pl.BlockSpec((None, be, d), lambda i: (layer, i, 0))
    return pl.pallas_call(
        _pack_kernel,
        grid=(e // be,),
        in_specs=[spec, spec],
        out_specs=pl.BlockSpec((be, d), lambda i: (i, 0)),
        out_shape=jax.ShapeDtypeStruct((e, d), jnp.uint32),
        compiler_params=_cparams(("parallel",)),
        name="peer_pack",
    )(u_all, v_all)


PEER_RING = 8
PEER_AHEAD = PEER_RING - 1


def _peer_kernel(idx_ref, g_ref, gate_ref, res_ref, rgate_ref, tab_hbm, o_ref, *scratch, tokens, n_sel):
    bufs, sem = scratch[:PEER_RING], scratch[PEER_RING]
    step = pl.program_id(0)
    tile = 8

    def row_copy(e, s, k):
        return pltpu.make_async_copy(tab_hbm.at[pl.ds(e, 1), :], bufs[s].at[pl.ds(k, 1), :], sem.at[s])

    def wait(s):
        pltpu.make_async_copy(tab_hbm.at[pl.ds(0, n_sel), :], bufs[s], sem.at[s]).wait()

    @pl.when(step == 0)
    def _():
        for t in range(PEER_AHEAD):
            for k in range(n_sel):
                row_copy(idx_ref[t, k], t, k).start()

    eye = (lax.broadcasted_iota(jnp.int32, (n_sel, n_sel), 0)
           == lax.broadcasted_iota(jnp.int32, (n_sel, n_sel), 1))

    def process(t, s):
        cur, s_nxt = bufs[s], (s + PEER_AHEAD) % PEER_RING
        wait(s)
        g = g_ref[pl.ds(t, 1), :]
        parts = []
        for c in range(n_sel // tile):
            for k in range(c * tile, (c + 1) * tile):
                row_copy(idx_ref[t + PEER_AHEAD, k], s_nxt, k).start(priority=k % 2)
            u = lax.bitcast_convert_type(cur[c * tile:(c + 1) * tile, :] << 16, F32)
            parts.append(jnp.sum(u * g, axis=1, keepdims=True))
        d = jnp.concatenate(parts, axis=0)
        gate_row = gate_ref[pl.ds(t, 1), :]
        gate_col = jnp.sum(jnp.where(eye, gate_row, 0.0), axis=1, keepdims=True)
        w = gate_col * _gelu(d)
        acc = jnp.zeros((tile, g.shape[1]), F32)
        for c in range(n_sel // tile):
            v = lax.bitcast_convert_type(cur[c * tile:(c + 1) * tile, :] & jnp.uint32(HI16), F32)
            acc = acc + v * w[c * tile:(c + 1) * tile, :]
        y = jnp.sum(acc, axis=0, keepdims=True)
        o_ref[pl.ds(t, 1), :] = res_ref[pl.ds(t, 1), :] + rgate_ref[0] * y

    def ring(i, carry):
        for s in range(PEER_RING):
            process(PEER_RING * i + s, s)
        return carry

    lax.fori_loop(0, tokens // PEER_RING, ring, 0)

    @pl.when(step == pl.num_programs(0) - 1)
    def _():
        for s in range(PEER_AHEAD):
            wait(s)


def _peer_experts(idx, gate, g, tab, res, mod3, gate_chunk, rows):
    m, d = g.shape
    n_sel = idx.shape[1]
    tb = _pick(m, 128, 8)
    nb = m // tb
    assert tb % PEER_RING == 0 and PEER_AHEAD <= 8
    blocks = idx.reshape(nb, tb, n_sel)
    idx_ext = jnp.concatenate([blocks, jnp.roll(blocks[:, :8], -1, axis=0)], axis=1)
    return pl.pallas_call(
        functools.partial(_peer_kernel, tokens=tb, n_sel=n_sel),
        grid=(nb,),
        in_specs=[pl.BlockSpec((None, tb + 8, n_sel), lambda i: (i, 0, 0), memory_space=pltpu.SMEM),
                  pl.BlockSpec((tb, d), lambda i: (i, 0)),
                  pl.BlockSpec((tb, n_sel), lambda i: (i, 0)),
                  pl.BlockSpec((tb, d), lambda i: (i, 0)),
                  pl.BlockSpec((1, 1, d), lambda i: (rows.mod_row(i, tb), 0, gate_chunk)),
                  pl.BlockSpec(memory_space=pl.ANY)],
        out_specs=pl.BlockSpec((tb, d), lambda i: (i, 0)),
        out_shape=jax.ShapeDtypeStruct((m, d), F32),
        scratch_shapes=[pltpu.VMEM((n_sel, d), jnp.uint32)] * PEER_RING
        + [pltpu.SemaphoreType.DMA((PEER_RING,))],
        compiler_params=_cparams(("arbitrary",)),
        name="peer_experts",
    )(idx_ext, g, gate, res, mod3, tab)


def _rope_tables(seq, rot_dim, pad):
    rows = seq // GRID_W
    row = jnp.repeat(jnp.arange(rows, dtype=F32), GRID_W)
    col = jnp.tile(jnp.arange(GRID_W, dtype=F32), rows)
    n_freq = rot_dim // 4
    inv = ROPE_BASE ** (-jnp.arange(n_freq, dtype=F32) / n_freq)
    ar, ac = row[:, None] * inv, col[:, None] * inv
    ang = jnp.concatenate([ar, ar, ac, ac], axis=-1)
    sign = jnp.concatenate([-jnp.ones(n_freq), jnp.ones(n_freq)] * 2).astype(F32)
    cos, sin = jnp.cos(ang), jnp.sin(ang) * sign
    reps = LANE // rot_dim
    cos, sin = jnp.tile(cos, (1, reps)), jnp.tile(sin, (1, reps))
    cos = jnp.concatenate([cos, jnp.ones((pad, LANE), F32)], axis=0)
    sin = jnp.concatenate([sin, jnp.zeros((pad, LANE), F32)], axis=0)
    return cos, sin


def kernel(x, c, ctx, c_ctx, w_mod, b_mod, norm1_gain, norm2_gain, w_in, a_v_gain, a_w_s, a_b_s,
           b_q_gain, b_kv_gain, b_w_uq, b_w_ukv, b_qn_gain, b_kn_gain, c_qn_gain, c_kn_gain, c_sink,
           w_out, peer_w_q, peer_subkeys, peer_u, peer_v):
    batch, seq, d = x.shape
    ctx_len = ctx.shape[1]
    depth = w_mod.shape[0]
    a_width = a_v_gain.shape[1]
    q_lora, kv_lora = b_q_gain.shape[1], b_kv_gain.shape[1]
    b_heads = b_w_uq.shape[2] // B_QK
    c_heads = c_sink.shape[1]
    c_kv = c_heads // 3
    p_a = 2 * a_width
    p_b = q_lora + kv_lora + QK_ROPE
    p_b_pad = -(-p_b // (2 * LANE)) * (2 * LANE)
    assert b_heads % 2 == 0 and (q_lora + kv_lora) % LANE == 0
    assert seq % WINDOW == 0 and ctx_len % WINDOW == 0

    rows_all = _Rows(batch, seq, ctx_len)
    bm_tab = rows_all.block(256)
    cos_b, sin_b = _rope_tables(seq, QK_ROPE, bm_tab)
    cos_c, sin_c = _rope_tables(seq, HEAD_DIM, bm_tab)

    c8 = jnp.zeros((8, d), F32).at[:batch].set(c).at[batch].set(c_ctx)
    xs = jnp.concatenate([x.reshape(batch * seq, d), ctx.reshape(batch * ctx_len, d)], axis=0)

    for layer in range(depth):
        need_ctx = layer < depth - 1
        mod3 = _modulation(c8, w_mod, b_mod[layer], layer).reshape(8, 1, N_MOD * d)

        w_l = w_in[layer]
        w_a = w_l[:, :p_a].astype(BF16)
        w_b = jnp.pad(w_l[:, p_a:p_a + p_b], ((0, 0), (0, p_b_pad - p_b))).astype(BF16)
        w_c = w_l[:, p_a + p_b:].astype(BF16)
        w_uq = b_w_uq[layer].reshape(q_lora, b_heads, B_QK)
        w_uq = jnp.concatenate([w_uq[:, :, :QK_NOPE].reshape(q_lora, -1),
                                w_uq[:, :, QK_NOPE:].reshape(q_lora, -1)], axis=1).astype(BF16)
        w_ukv = b_w_ukv[layer].reshape(kv_lora, b_heads, QK_NOPE + HEAD_DIM)
        w_ukv = jnp.concatenate([w_ukv[:, :, :QK_NOPE].reshape(kv_lora, -1),
                                 w_ukv[:, :, QK_NOPE:].reshape(kv_lora, -1)], axis=1).astype(BF16)

        (h,) = _norm_mod(xs, norm1_gain[layer], mod3, 0, 1, rows_all, [BF16])
        pa = _matmul(h, w_a)
        pb = _matmul(h, w_b)
        pc = _matmul(h, w_c, bn_target=1280)

        oa = _gmlp(pa, a_v_gain[layer], a_w_s[layer], a_b_s[layer])

        cq, ckv = _mla_pre(pb, b_q_gain[layer], b_kv_gain[layer], rows_all)
        q_raw = _matmul(cq, w_uq)
        kv_raw = _matmul(ckv, w_ukv)
        qh, kh, vh = _mla_post(q_raw, kv_raw, pb, (q_lora + kv_lora) // LANE, b_qn_gain[layer],
                               b_kn_gain[layer], cos_b, sin_b, rows_all, b_heads)
        ob = _mla_attention(qh, kh, vh, rows_all, b_heads, True)

        qc, kc, vc = _gqa_post(pc, c_qn_gain[layer], c_kn_gain[layer], cos_c, sin_c, rows_all,
                               c_heads, c_kv)
        oc = _gqa_attention(qc, kc, vc, c_sink[layer], rows_all, c_heads, c_kv, True)

        if need_ctx:
            rows = rows_all
            ob = jnp.concatenate([ob, _mla_attention(qh, kh, vh, rows_all, b_heads, False)], axis=0)
            oc = jnp.concatenate(
                [oc, _gqa_attention(qc, kc, vc, c_sink[layer], rows_all, c_heads, c_kv, False)], axis=0)
        else:
            rows = _Rows(batch, seq, 0)
            oa = oa[:rows.m_lat]
            xs = xs[:rows.m_lat]
        w_o = w_out[layer]
        n_b = b_heads * HEAD_DIM
        w_parts = [w_o[:a_width].astype(BF16), w_o[a_width:a_width + n_b].astype(BF16),
                   w_o[a_width + n_b:].astype(BF16)]
        xs = _matmul_residual([oa, ob, oc], w_parts, xs, mod3, 2, rows)

        g_bf, g = _norm_mod(xs, norm2_gain[layer], mod3, 3, 4, rows, [BF16, F32])
        pq = _matmul(g_bf, peer_w_q[layer].astype(BF16))
        idx, gate = _peer_route(pq, peer_subkeys[layer])
        n_sel = idx.shape[0] * idx.shape[1]
        idx = idx.reshape(n_sel, -1).T
        gate = gate.reshape(n_sel, -1).T
        xs = _peer_experts(idx, gate, g, _pack_tables(peer_u, peer_v, layer), xs, mod3, 5, rows)

    return xs[:batch * seq].reshape(batch, seq, d)
```

```python
import functools

import jax
import jax.numpy as jnp
from jax import lax
from jax.experimental import pallas as pl
from jax.experimental.pallas import tpu as pltpu

F32 = jnp.float32
BF16 = jnp.bfloat16

LANE = 128
HEAD_DIM = 128
QK_NOPE = 128
QK_ROPE = 64
B_QK = QK_NOPE + QK_ROPE
B_QK_PAD = 2 * LANE
GRID_W = 64
WINDOW = 128
CHUNK = 128
PEER_TOPK = 16
N_MOD = 6
ROPE_BASE = 10000.0
EPS = 1e-6
NEG = -1e30
LOG2E = 1.4426950408889634
MIB = 1024 * 1024
VMEM_LIMIT = 56 * MIB


def _cparams(semantics, vmem=VMEM_LIMIT):
    return pltpu.CompilerParams(dimension_semantics=semantics, vmem_limit_bytes=vmem)


def _pick(n, target, align=LANE):
    if n <= target:
        return n
    best = None
    d = align
    while d <= target:
        if n % d == 0:
            best = d
        d += align
    assert best is not None, (n, target, align)
    return best


def _gelu(x):
    return 0.5 * x * (1.0 + jnp.tanh(0.7978845608028654 * (x + 0.044715 * (x * x * x))))


def _dot_nt(a, b):
    return lax.dot_general(a, b, (((1,), (1,)), ((), ())), preferred_element_type=F32)


def _modulation_kernel(c_ref, w_ref, b_ref, o_ref, acc_ref):
    k = pl.program_id(1)

    @pl.when(k == 0)
    def _():
        acc_ref[...] = jnp.zeros_like(acc_ref)

    c = c_ref[...]
    silu = c / (1.0 + jnp.exp(-c))
    acc_ref[...] += jnp.dot(silu.astype(BF16), w_ref[...].astype(BF16), preferred_element_type=F32)

    @pl.when(k == pl.num_programs(1) - 1)
    def _():
        o_ref[...] = acc_ref[...] + b_ref[...]


def _modulation(c8, w_all, b, layer):
    rows, d = c8.shape
    n = w_all.shape[2]
    bn = _pick(n, 2048)
    bk = _pick(d, 1024)
    return pl.pallas_call(
        _modulation_kernel,
        grid=(n // bn, d // bk),
        in_specs=[pl.BlockSpec((rows, bk), lambda j, k: (0, k)),
                  pl.BlockSpec((None, bk, bn), lambda j, k: (layer, k, j)),
                  pl.BlockSpec((1, bn), lambda j, k: (0, j))],
        out_specs=pl.BlockSpec((rows, bn), lambda j, k: (0, j)),
        out_shape=jax.ShapeDtypeStruct((rows, n), F32),
        scratch_shapes=[pltpu.VMEM((rows, bn), F32)],
        compiler_params=_cparams(("parallel", "arbitrary")),
        name="modulation",
    )(c8, w_all, b.reshape(1, n))


class _Rows:
    def __init__(self, batch, seq, ctx_len):
        self.batch, self.seq, self.ctx_len = batch, seq, ctx_len
        self.m_lat = batch * seq
        self.m_ctx = batch * ctx_len
        self.m = self.m_lat + self.m_ctx

    def block(self, target):
        bm = target
        while self.seq % bm or self.m_ctx % bm:
            bm //= 2
        assert bm >= 8
        return bm

    def mod_row(self, i, bm):
        return jnp.where(i < self.m_lat // bm, (i * bm) // self.seq, self.batch)


def _norm_mod_kernel(x_ref, gain_ref, shift_ref, scale_ref, *out_refs):
    x = x_ref[...]
    y = x * lax.rsqrt(jnp.mean(x * x, axis=-1, keepdims=True) + EPS) * gain_ref[...]
    h = y * (1.0 + scale_ref[0]) + shift_ref[0]
    for o in out_refs:
        o[...] = h.astype(o.dtype)


def _norm_mod(x, gain, mod3, shift_chunk, scale_chunk, rows, out_dtypes):
    m, d = x.shape
    bm = rows.block(256)
    outs = pl.pallas_call(
        _norm_mod_kernel,
        grid=(m // bm,),
        in_specs=[pl.BlockSpec((bm, d), lambda i: (i, 0)),
                  pl.BlockSpec((1, d), lambda i: (0, 0)),
                  pl.BlockSpec((1, 1, d), lambda i: (rows.mod_row(i, bm), 0, shift_chunk)),
                  pl.BlockSpec((1, 1, d), lambda i: (rows.mod_row(i, bm), 0, scale_chunk))],
        out_specs=[pl.BlockSpec((bm, d), lambda i: (i, 0)) for _ in out_dtypes],
        out_shape=[jax.ShapeDtypeStruct((m, d), dt) for dt in out_dtypes],
        compiler_params=_cparams(("parallel",)),
        name="norm_mod",
    )(x, gain.reshape(1, d), mod3, mod3)
    return outs


def _mm_kernel(a_ref, w_ref, o_ref):
    o_ref[...] = jnp.dot(a_ref[...], w_ref[...], preferred_element_type=F32).astype(o_ref.dtype)


def _mm_res_kernel(*refs, parts):
    a_refs, w_refs = refs[:parts], refs[parts:2 * parts]
    res_ref, gate_ref, o_ref = refs[2 * parts:]
    acc = jnp.dot(a_refs[0][...], w_refs[0][...], preferred_element_type=F32)
    for a_ref, w_ref in zip(a_refs[1:], w_refs[1:]):
        acc = acc + jnp.dot(a_ref[...], w_ref[...], preferred_element_type=F32)
    o_ref[...] = res_ref[...] + gate_ref[0] * acc


def _matmul(a, w, out_dtype=F32, bm_target=1024, bn_target=1024):
    m, k = a.shape
    n = w.shape[1]
    bm = _pick(m, bm_target, 8)
    bn = _pick(n, bn_target)
    return pl.pallas_call(
        _mm_kernel,
        grid=(m // bm, n // bn),
        in_specs=[pl.BlockSpec((bm, k), lambda i, j: (i, 0)),
                  pl.BlockSpec((k, bn), lambda i, j: (0, j))],
        out_specs=pl.BlockSpec((bm, bn), lambda i, j: (i, j)),
        out_shape=jax.ShapeDtypeStruct((m, n), out_dtype),
        compiler_params=_cparams(("parallel", "parallel")),
        name="matmul",
    )(a, w)


def _gated_add_kernel(res_ref, y_ref, gate_ref, o_ref):
    o_ref[...] = res_ref[...] + gate_ref[0] * y_ref[...]


def _gated_add(res, y, mod3, gate_chunk, rows):
    m, d = y.shape
    bm = rows.block(256)
    return pl.pallas_call(
        _gated_add_kernel,
        grid=(m // bm,),
        in_specs=[pl.BlockSpec((bm, d), lambda i: (i, 0)),
                  pl.BlockSpec((bm, d), lambda i: (i, 0)),
                  pl.BlockSpec((1, 1, d), lambda i: (rows.mod_row(i, bm), 0, gate_chunk))],
        out_specs=pl.BlockSpec((bm, d), lambda i: (i, 0)),
        out_shape=jax.ShapeDtypeStruct((m, d), F32),
        compiler_params=_cparams(("parallel",)),
        name="gated_add",
    )(res, y, mod3)


def _matmul_residual(a_parts, w_parts, res, mod3, gate_chunk, rows):
    m, n = rows.m, res.shape[1]
    bm = rows.block(1024)
    bn = _pick(n, 512)
    parts = len(a_parts)
    a_specs = [pl.BlockSpec((bm, a.shape[1]), lambda i, j: (i, 0)) for a in a_parts]
    w_specs = [pl.BlockSpec((w.shape[0], bn), lambda i, j: (0, j)) for w in w_parts]
    return pl.pallas_call(
        functools.partial(_mm_res_kernel, parts=parts),
        grid=(m // bm, n // bn),
        in_specs=a_specs + w_specs + [
            pl.BlockSpec((bm, bn), lambda i, j: (i, j)),
            pl.BlockSpec((1, 1, bn), lambda i, j: (rows.mod_row(i, bm), 0, gate_chunk * (n // bn) + j))],
        out_specs=pl.BlockSpec((bm, bn), lambda i, j: (i, j)),
        out_shape=jax.ShapeDtypeStruct((m, n), F32),
        compiler_params=_cparams(("parallel", "parallel")),
        name="matmul_residual",
    )(*a_parts, *w_parts, res, mod3)


def _gmlp_kernel(p_ref, gain_ref, ws_ref, bs_ref, o_ref, *, width, chunks):
    ws = ws_ref[...].astype(BF16)
    bs = bs_ref[...]
    for c in range(chunks):
        r0 = c * CHUNK
        z = _gelu(p_ref[r0:r0 + CHUNK, :])
        u = z[:, :width]
        parts = []
        for h in range(width // HEAD_DIM):
            vh = z[:, width + h * HEAD_DIM: width + (h + 1) * HEAD_DIM]
            vh = vh * lax.rsqrt(jnp.mean(vh * vh, axis=-1, keepdims=True) + EPS)
            parts.append((vh * gain_ref[:, h * HEAD_DIM:(h + 1) * HEAD_DIM]).astype(BF16))
        vn = jnp.concatenate(parts, axis=1)
        s = jnp.dot(ws, vn, preferred_element_type=F32) + bs
        o_ref[r0:r0 + CHUNK, :] = (u * s).astype(o_ref.dtype)


def _gmlp(pa, v_gain, w_s, b_s):
    m, two_w = pa.shape
    width = two_w // 2
    chunks = 2
    bm = chunks * CHUNK
    return pl.pallas_call(
        functools.partial(_gmlp_kernel, width=width, chunks=chunks),
        grid=(m // bm,),
        in_specs=[pl.BlockSpec((bm, two_w), lambda i: (i, 0)),
                  pl.BlockSpec((1, width), lambda i: (0, 0)),
                  pl.BlockSpec((CHUNK, CHUNK), lambda i: (0, 0)),
                  pl.BlockSpec((CHUNK, 1), lambda i: (0, 0))],
        out_specs=pl.BlockSpec((bm, width), lambda i: (i, 0)),
        out_shape=jax.ShapeDtypeStruct((m, width), BF16),
        compiler_params=_cparams(("parallel",)),
        name="gmlp",
    )(pa, v_gain.reshape(1, width), w_s, b_s.reshape(CHUNK, 1))


def _rope(x, cos, sin_signed, half):
    lane = lax.broadcasted_iota(jnp.int32, x.shape, 1)
    first = (lane & (2 * half - 1)) < half
    partner = jnp.where(first, pltpu.roll(x, LANE - half, axis=1), pltpu.roll(x, half, axis=1))
    return x * cos + partner * sin_signed


def _mla_pre_kernel(p_ref, qg_ref, kvg_ref, cq_ref, ckv_ref, *, q_lora, kv_lora):
    cq = p_ref[:, :q_lora]
    cq_ref[...] = (cq * lax.rsqrt(jnp.mean(cq * cq, axis=-1, keepdims=True) + EPS)
                   * qg_ref[...]).astype(cq_ref.dtype)
    ckv = p_ref[:, q_lora:q_lora + kv_lora]
    ckv_ref[...] = (ckv * lax.rsqrt(jnp.mean(ckv * ckv, axis=-1, keepdims=True) + EPS)
                    * kvg_ref[...]).astype(ckv_ref.dtype)


def _mla_pre(pb, q_gain, kv_gain, rows):
    m, n = pb.shape
    q_lora, kv_lora = q_gain.shape[0], kv_gain.shape[0]
    bm = rows.block(256)
    return pl.pallas_call(
        functools.partial(_mla_pre_kernel, q_lora=q_lora, kv_lora=kv_lora),
        grid=(m // bm,),
        in_specs=[pl.BlockSpec((bm, n), lambda i: (i, 0)),
                  pl.BlockSpec((1, q_lora), lambda i: (0, 0)),
                  pl.BlockSpec((1, kv_lora), lambda i: (0, 0))],
        out_specs=[pl.BlockSpec((bm, q_lora), lambda i: (i, 0)),
                   pl.BlockSpec((bm, kv_lora), lambda i: (i, 0))],
        out_shape=[jax.ShapeDtypeStruct((m, q_lora), BF16),
                   jax.ShapeDtypeStruct((m, kv_lora), BF16)],
        compiler_params=_cparams(("parallel",)),
        name="mla_pre",
    )(pb, q_gain.reshape(1, q_lora), kv_gain.reshape(1, kv_lora))


def _mla_post_kernel(q_ref, kv_ref, kr_ref, qgn_ref, qgr_ref, kgn_ref, kgr_ref, cos_ref, sin_ref,
                     qh_ref, kh_ref, vh_ref, *, heads):
    cos, sin = cos_ref[...], sin_ref[...]
    lane = lax.broadcasted_iota(jnp.int32, cos.shape, 1)
    low = lane < QK_ROPE
    half = QK_ROPE // 4
    zeros = jnp.zeros(cos.shape, F32)

    kr = jnp.where(low, kr_ref[...], 0.0)
    kr_ss = jnp.sum(kr * kr, axis=-1, keepdims=True)
    kr_rot = _rope(kr * kgr_ref[...], cos, sin, half)

    for hp in range(heads // 2):
        qr = q_ref[:, heads * QK_NOPE + hp * LANE: heads * QK_NOPE + (hp + 1) * LANE]
        qr2 = qr * qr
        ss_lo = jnp.sum(jnp.where(low, qr2, 0.0), axis=-1, keepdims=True)
        ss_hi = jnp.sum(jnp.where(low, 0.0, qr2), axis=-1, keepdims=True)
        rinv = []
        for j, ss_r in enumerate((ss_lo, ss_hi)):
            h = 2 * hp + j
            qn = q_ref[:, h * QK_NOPE:(h + 1) * QK_NOPE]
            r = lax.rsqrt((jnp.sum(qn * qn, axis=-1, keepdims=True) + ss_r) * (1.0 / B_QK) + EPS)
            rinv.append(r)
            qh_ref[h, :, :QK_NOPE] = (qn * r * qgn_ref[...]).astype(qh_ref.dtype)
        qrot = _rope(qr * jnp.where(low, rinv[0], rinv[1]) * qgr_ref[...], cos, sin, half)
        qh_ref[2 * hp, :, QK_NOPE:] = jnp.where(low, qrot, zeros).astype(qh_ref.dtype)
        qh_ref[2 * hp + 1, :, QK_NOPE:] = jnp.where(
            low, pltpu.roll(qrot, QK_ROPE, axis=1), zeros).astype(qh_ref.dtype)

    for h in range(heads):
        kn = kv_ref[:, h * QK_NOPE:(h + 1) * QK_NOPE]
        r = lax.rsqrt((jnp.sum(kn * kn, axis=-1, keepdims=True) + kr_ss) * (1.0 / B_QK) + EPS)
        kh_ref[h, :, :QK_NOPE] = (kn * r * kgn_ref[...]).astype(kh_ref.dtype)
        kh_ref[h, :, QK_NOPE:] = (kr_rot * r).astype(kh_ref.dtype)
        vh_ref[h] = kv_ref[:, (heads + h) * HEAD_DIM:(heads + h + 1) * HEAD_DIM].astype(vh_ref.dtype)


def _mla_post(q_raw, kv_raw, pb, kr_block, qn_gain, kn_gain, cos_t, sin_t, rows, heads):
    m = q_raw.shape[0]
    bm = rows.block(256)
    n_lat, n_tab = rows.m_lat // bm, rows.seq // bm
    tab = lambda i: (jnp.where(i < n_lat, i % n_tab, n_tab), 0)
    pair = lambda g: jnp.concatenate([g, g]).reshape(1, LANE)
    return pl.pallas_call(
        functools.partial(_mla_post_kernel, heads=heads),
        grid=(m // bm,),
        in_specs=[pl.BlockSpec((bm, q_raw.shape[1]), lambda i: (i, 0)),
                  pl.BlockSpec((bm, kv_raw.shape[1]), lambda i: (i, 0)),
                  pl.BlockSpec((bm, LANE), lambda i: (i, kr_block)),
                  pl.BlockSpec((1, QK_NOPE), lambda i: (0, 0)),
                  pl.BlockSpec((1, LANE), lambda i: (0, 0)),
                  pl.BlockSpec((1, QK_NOPE), lambda i: (0, 0)),
                  pl.BlockSpec((1, LANE), lambda i: (0, 0)),
                  pl.BlockSpec((bm, LANE), tab),
                  pl.BlockSpec((bm, LANE), tab)],
        out_specs=[pl.BlockSpec((heads, bm, B_QK_PAD), lambda i: (0, i, 0)),
                   pl.BlockSpec((heads, bm, B_QK_PAD), lambda i: (0, i, 0)),
                   pl.BlockSpec((heads, bm, HEAD_DIM), lambda i: (0, i, 0))],
        out_shape=[jax.ShapeDtypeStruct((heads, m, B_QK_PAD), BF16),
                   jax.ShapeDtypeStruct((heads, m, B_QK_PAD), BF16),
                   jax.ShapeDtypeStruct((heads, m, HEAD_DIM), BF16)],
        compiler_params=_cparams(("parallel",)),
        name="mla_post",
    )(q_raw, kv_raw, pb, qn_gain[:QK_NOPE].reshape(1, QK_NOPE), pair(qn_gain[QK_NOPE:]),
      kn_gain[:QK_NOPE].reshape(1, QK_NOPE), pair(kn_gain[QK_NOPE:]), cos_t, sin_t)


def _mla_attn_kernel(q_ref, kc_ref, vc_ref, *rest, with_lat):
    if with_lat:
        kl_ref, vl_ref, o_ref = rest
    else:
        (o_ref,) = rest
    c = (B_QK ** -0.5) * LOG2E
    tn = (((0,), (0,)), ((), ()))
    q = q_ref[...]
    s_c = _dot_nt(kc_ref[...], q)
    m = jnp.max(s_c, axis=0, keepdims=True)
    if with_lat:
        s_l = _dot_nt(kl_ref[...], q)
        m = jnp.maximum(m, jnp.max(s_l, axis=0, keepdims=True))
    p_c = jnp.exp2((s_c - m) * c)
    den = jnp.sum(p_c, axis=0, keepdims=True)
    acc = lax.dot_general(vc_ref[...], p_c.astype(BF16), tn, preferred_element_type=F32)
    if with_lat:
        p_l = jnp.exp2((s_l - m) * c)
        den = den + jnp.sum(p_l, axis=0, keepdims=True)
        acc = acc + lax.dot_general(vl_ref[...], p_l.astype(BF16), tn, preferred_element_type=F32)
    o_ref[...] = (acc / den).T.astype(o_ref.dtype)


def _mla_attention(qh, kh, vh, rows, heads, latent_queries):
    b, s, c = rows.batch, rows.seq, rows.ctx_len
    ctx_blk0 = rows.m_lat // c
    if latent_queries:
        tq = _pick(s, 1024, 8)
        nq = s // tq
        q_map = lambda bi, h, i: (h, bi * nq + i, 0)
        o_map = lambda bi, h, i: (bi * nq + i, h)
        m_out = rows.m_lat
    else:
        tq, nq = c, 1
        q_map = lambda bi, h, i: (h, ctx_blk0 + bi, 0)
        o_map = lambda bi, h, i: (bi, h)
        m_out = rows.m_ctx
    in_specs = [pl.BlockSpec((None, tq, B_QK_PAD), q_map),
                pl.BlockSpec((None, c, B_QK_PAD), lambda bi, h, i: (h, ctx_blk0 + bi, 0)),
                pl.BlockSpec((None, c, HEAD_DIM), lambda bi, h, i: (h, ctx_blk0 + bi, 0))]
    args = [qh, kh, vh]
    if latent_queries:
        in_specs += [pl.BlockSpec((None, s, B_QK_PAD), lambda bi, h, i: (h, bi, 0)),
                     pl.BlockSpec((None, s, HEAD_DIM), lambda bi, h, i: (h, bi, 0))]
        args += [kh, vh]
    return pl.pallas_call(
        functools.partial(_mla_attn_kernel, with_lat=latent_queries),
        grid=(b, heads, nq),
        in_specs=in_specs,
        out_specs=pl.BlockSpec((tq, HEAD_DIM), o_map),
        out_shape=jax.ShapeDtypeStruct((m_out, heads * HEAD_DIM), BF16),
        compiler_params=_cparams(("parallel", "parallel", "parallel")),
        name="mla_attention",
    )(*args)


def _gqa_post_kernel(p_ref, qg_ref, kg_ref, cos_ref, sin_ref, q_ref, k_ref, v_ref, *, q_heads, kv_heads):
    cos, sin = cos_ref[...], sin_ref[...]
    half = HEAD_DIM // 4

    def norm_rope(x, gain):
        xn = x * lax.rsqrt(jnp.mean(x * x, axis=-1, keepdims=True) + EPS) * gain
        return _rope(xn, cos, sin, half)

    for h in range(q_heads):
        sl = slice(h * HEAD_DIM, (h + 1) * HEAD_DIM)
        q_ref[:, sl] = norm_rope(p_ref[:, sl], qg_ref[...]).astype(q_ref.dtype)
    for h in range(kv_heads):
        sl = slice(h * HEAD_DIM, (h + 1) * HEAD_DIM)
        k0 = q_heads * HEAD_DIM
        v0 = (q_heads + kv_heads) * HEAD_DIM
        k_ref[:, sl] = norm_rope(p_ref[:, k0 + h * HEAD_DIM:k0 + (h + 1) * HEAD_DIM],
                                 kg_ref[...]).astype(k_ref.dtype)
        v_ref[:, sl] = p_ref[:, v0 + h * HEAD_DIM:v0 + (h + 1) * HEAD_DIM].astype(v_ref.dtype)


def _gqa_post(pc, qn_gain, kn_gain, cos_t, sin_t, rows, q_heads, kv_heads):
    m, n = pc.shape
    bm = rows.block(256)
    n_lat, n_tab = rows.m_lat // bm, rows.seq // bm
    tab = lambda i: (jnp.where(i < n_lat, i % n_tab, n_tab), 0)
    return pl.pallas_call(
        functools.partial(_gqa_post_kernel, q_heads=q_heads, kv_heads=kv_heads),
        grid=(m // bm,),
        in_specs=[pl.BlockSpec((bm, n), lambda i: (i, 0)),
                  pl.BlockSpec((1, HEAD_DIM), lambda i: (0, 0)),
                  pl.BlockSpec((1, HEAD_DIM), lambda i: (0, 0)),
                  pl.BlockSpec((bm, LANE), tab),
                  pl.BlockSpec((bm, LANE), tab)],
        out_specs=[pl.BlockSpec((bm, q_heads * HEAD_DIM), lambda i: (i, 0)),
                   pl.BlockSpec((bm, kv_heads * HEAD_DIM), lambda i: (i, 0)),
                   pl.BlockSpec((bm, kv_heads * HEAD_DIM), lambda i: (i, 0))],
        out_shape=[jax.ShapeDtypeStruct((m, q_heads * HEAD_DIM), BF16),
                   jax.ShapeDtypeStruct((m, kv_heads * HEAD_DIM), BF16),
                   jax.ShapeDtypeStruct((m, kv_heads * HEAD_DIM), BF16)],
        compiler_params=_cparams(("parallel",)),
        name="gqa_post",
    )(pc, qn_gain.reshape(1, HEAD_DIM), kn_gain.reshape(1, HEAD_DIM), cos_t, sin_t)


def _gqa_attn_kernel(sink_ref, q_ref, kc_ref, vc_ref, *rest, group, seq, banded):
    if banded:
        kp_ref, kq_ref, kn_ref, vp_ref, vq_ref, vn_ref, o_ref = rest
    else:
        (o_ref,) = rest
    scale = HEAD_DIM ** -0.5
    blk = pl.program_id(1)
    rows_q = group * WINDOW
    kv_heads = kc_ref.shape[1] // HEAD_DIM
    row = lax.broadcasted_iota(jnp.int32, (rows_q, 1), 0)

    for kvh in range(kv_heads):
        hs = slice(kvh * HEAD_DIM, (kvh + 1) * HEAD_DIM)
        q0 = kvh * group * HEAD_DIM
        q = jnp.concatenate([q_ref[:, q0 + g * HEAD_DIM:q0 + (g + 1) * HEAD_DIM]
                             for g in range(group)], axis=0)
        sink = jnp.zeros((rows_q, 1), F32)
        for g in range(group):
            in_g = (row >= g * WINDOW) & (row < (g + 1) * WINDOW)
            sink = jnp.where(in_g, sink_ref[kvh * group + g], sink)

        s_c = _dot_nt(q, kc_ref[:, hs]) * scale
        m = jnp.maximum(jnp.max(s_c, axis=-1, keepdims=True), sink)
        if banded:
            kb = jnp.concatenate([kp_ref[:, hs], kq_ref[:, hs], kn_ref[:, hs]], axis=0)
            vb = jnp.concatenate([vp_ref[:, hs], vq_ref[:, hs], vn_ref[:, hs]], axis=0)
            s_b = _dot_nt(q, kb) * scale
            qpos = lax.broadcasted_iota(jnp.int32, s_b.shape, 0) & (WINDOW - 1)
            krel = lax.broadcasted_iota(jnp.int32, s_b.shape, 1) - WINDOW
            kpos = blk * WINDOW + krel
            valid = (jnp.abs(qpos - krel) <= WINDOW) & (kpos >= 0) & (kpos < seq)
            s_b = jnp.where(valid, s_b, NEG)
            m = jnp.maximum(m, jnp.max(s_b, axis=-1, keepdims=True))
        p_c = jnp.exp(s_c - m)
        den = jnp.sum(p_c, axis=-1, keepdims=True) + jnp.exp(sink - m)
        acc = jnp.dot(p_c.astype(BF16), vc_ref[:, hs], preferred_element_type=F32)
        if banded:
            p_b = jnp.exp(s_b - m)
            den = den + jnp.sum(p_b, axis=-1, keepdims=True)
            acc = acc + jnp.dot(p_b.astype(BF16), vb, preferred_element_type=F32)
        o = acc / den
        for g in range(group):
            o_ref[:, q0 + g * HEAD_DIM:q0 + (g + 1) * HEAD_DIM] = (
                o[g * WINDOW:(g + 1) * WINDOW].astype(o_ref.dtype))


def _gqa_attention(qc, kc, vc, sink, rows, q_heads, kv_heads, latent_queries):
    b, s, c = rows.batch, rows.seq, rows.ctx_len
    group = q_heads // kv_heads
    qw, kw = q_heads * HEAD_DIM, kv_heads * HEAD_DIM
    ctx_blk0 = rows.m_lat // c
    ctx_spec = lambda: pl.BlockSpec((c, kw), lambda bi, n: (ctx_blk0 + bi, 0))
    if latent_queries:
        nb = s // WINDOW
        qrow = lambda bi, n: bi * nb + n
        m_out = rows.m_lat
        orow = qrow
    else:
        nb = c // WINDOW
        qrow = lambda bi, n: rows.m_lat // WINDOW + bi * nb + n
        orow = lambda bi, n: bi * nb + n
        m_out = rows.m_ctx
    in_specs = [pl.BlockSpec(memory_space=pltpu.SMEM),
                pl.BlockSpec((WINDOW, qw), lambda bi, n: (qrow(bi, n), 0)),
                ctx_spec(), ctx_spec()]
    args = [sink, qc, kc, vc]
    if latent_queries:
        band = [lambda bi, n: (bi * nb + jnp.maximum(n - 1, 0), 0),
                lambda bi, n: (bi * nb + n, 0),
                lambda bi, n: (bi * nb + jnp.minimum(n + 1, nb - 1), 0)]
        in_specs += [pl.BlockSpec((WINDOW, kw), f) for f in band] * 2
        args += [kc, kc, kc, vc, vc, vc]
    return pl.pallas_call(
        functools.partial(_gqa_attn_kernel, group=group, seq=s, banded=latent_queries),
        grid=(b, nb),
        in_specs=in_specs,
        out_specs=pl.BlockSpec((WINDOW, qw), lambda bi, n: (orow(bi, n), 0)),
        out_shape=jax.ShapeDtypeStruct((m_out, qw), BF16),
        compiler_params=_cparams(("parallel", "parallel")),
        name="gqa_attention",
    )(*args)


def _extract_topk(s, pos, ids, count, val_ref, id_ref):
    beyond = 3.0e38
    for r in range(count):
        top = jnp.max(s, axis=0, keepdims=True)
        first = jnp.min(jnp.where(s == top, pos, beyond), axis=0, keepdims=True)
        hit = pos == first
        val_ref[r:r + 1, :] = top
        if ids is None:
            id_ref[r:r + 1, :] = first
        else:
            id_ref[r:r + 1, :] = jnp.max(jnp.where(hit, ids, -1.0), axis=0, keepdims=True)
        s = jnp.where(hit, -jnp.inf, s)


_PAIR_SPANS = [(a, PEER_TOPK // (a + 1)) for a in range(PEER_TOPK)]
_N_PAIRS = sum(n for _, n in _PAIR_SPANS)
_N_PAIRS_PAD = -(-_N_PAIRS // 8) * 8


def _route_kernel(q_ref, sk_ref, idx_ref, gate_ref, tv_ref, ti_ref, cs_ref, cp_ref, ci_ref, bs_ref,
                  bi_ref, *, n_keys):
    tt = q_ref.shape[0]
    key_id = lax.broadcasted_iota(jnp.int32, (n_keys, tt), 0).astype(F32)
    for p in range(2):
        qp = q_ref[:, p * LANE:(p + 1) * LANE].astype(BF16)
        scores = _dot_nt(sk_ref[p], qp)
        _extract_topk(scores, key_id, None, PEER_TOPK, tv_ref.at[p], ti_ref.at[p])
    cs_ref[...] = jnp.full(cs_ref.shape, -jnp.inf, F32)
    cp_ref[...] = jnp.full(cp_ref.shape, 1.0e6, F32)
    ci_ref[...] = jnp.full(ci_ref.shape, -1.0, F32)
    off = 0
    for a, n in _PAIR_SPANS:
        rs = slice(off, off + n)
        cs_ref[rs, :] = tv_ref[0, a:a + 1, :] + tv_ref[1, 0:n, :]
        cp_ref[rs, :] = float(a * PEER_TOPK) + lax.broadcasted_iota(jnp.int32, (n, tt), 0).astype(F32)
        ci_ref[rs, :] = ti_ref[0, a:a + 1, :] * float(n_keys) + ti_ref[1, 0:n, :]
        off += n
    _extract_topk(cs_ref[...], cp_ref[...], ci_ref[...], PEER_TOPK, bs_ref, bi_ref)
    best = bs_ref[...]
    e = jnp.exp(best - best[0:1, :])
    gate_ref[...] = e / jnp.sum(e, axis=0, keepdims=True)
    idx_ref[...] = bi_ref[...].astype(jnp.int32)


def _peer_route(pq, subkeys):
    m = pq.shape[0]
    heads, _, n_keys, half = subkeys.shape
    assert half == LANE
    tt = _pick(m, 1024)
    k = PEER_TOPK
    return pl.pallas_call(
        functools.partial(_route_kernel, n_keys=n_keys),
        grid=(m // tt, heads),
        in_specs=[pl.BlockSpec((tt, 2 * LANE), lambda i, h: (i, h)),
                  pl.BlockSpec((None, 2, n_keys, half), lambda i, h: (h, 0, 0, 0))],
        out_specs=[pl.BlockSpec((None, k, tt), lambda i, h: (h, 0, i)),
                   pl.BlockSpec((None, k, tt), lambda i, h: (h, 0, i))],
        out_shape=[jax.ShapeDtypeStruct((heads, k, m), jnp.int32),
                   jax.ShapeDtypeStruct((heads, k, m), F32)],
        scratch_shapes=[pltpu.VMEM((2, k, tt), F32), pltpu.VMEM((2, k, tt), F32)]
        + [pltpu.VMEM((_N_PAIRS_PAD, tt), F32)] * 3
        + [pltpu.VMEM((k, tt), F32), pltpu.VMEM((k, tt), F32)],
        compiler_params=_cparams(("parallel", "parallel")),
        name="peer_route",
    )(pq, subkeys.astype(BF16))


HI16 = 0xFFFF0000


def _pack_kernel(u_ref, v_ref, o_ref):
    ub = lax.bitcast_convert_type(u_ref[...].astype(BF16).astype(F32), jnp.uint32)
    vb = lax.bitcast_convert_type(v_ref[...].astype(BF16).astype(F32), jnp.uint32)
    o_ref[...] = (vb & jnp.uint32(HI16)) | (ub >> 16)


def _pack_tables(u_all, v_all, layer):
    _, e, d = u_all.shape
    be = _pick(e, 256, 8)
    spec = pl.BlockSpec((None, be, d), lambda i: (layer, i, 0))
    return pl.pallas_call(
        _pack_kernel,
        grid=(e // be,),
        in_specs=[spec, spec],
        out_specs=pl.BlockSpec((be, d), lambda i: (i, 0)),
        out_shape=jax.ShapeDtypeStruct((e, d), jnp.uint32),
        compiler_params=_cparams(("parallel",)),
        name="peer_pack",
    )(u_all, v_all)


PEER_RING = 8
PEER_AHEAD = PEER_RING - 1


def _peer_kernel(idx_ref, g_ref, gate_ref, tab_hbm, o_ref, *scratch, tokens, n_sel):
    bufs, sem = scratch[:PEER_RING], scratch[PEER_RING]
    step = pl.program_id(0)
    tile = 8

    def row_copy(e, s, k):
        return pltpu.make_async_copy(tab_hbm.at[pl.ds(e, 1), :], bufs[s].at[pl.ds(k, 1), :], sem.at[s])

    def wait(s):
        pltpu.make_async_copy(tab_hbm.at[pl.ds(0, n_sel), :], bufs[s], sem.at[s]).wait()

    @pl.when(step == 0)
    def _():
        for t in range(PEER_AHEAD):
            for k in range(n_sel):
                row_copy(idx_ref[t, k], t, k).start()

    eye = (lax.broadcasted_iota(jnp.int32, (n_sel, n_sel), 0)
           == lax.broadcasted_iota(jnp.int32, (n_sel, n_sel), 1))

    def process(t, s):
        cur, s_nxt = bufs[s], (s + PEER_AHEAD) % PEER_RING
        wait(s)
        g = g_ref[pl.ds(t, 1), :]
        parts = []
        for c in range(n_sel // tile):
            for k in range(c * tile, (c + 1) * tile):
                row_copy(idx_ref[t + PEER_AHEAD, k], s_nxt, k).start(priority=k % 2)
            u = lax.bitcast_convert_type(cur[c * tile:(c + 1) * tile, :] << 16, F32)
            parts.append(jnp.sum(u * g, axis=1, keepdims=True))
        d = jnp.concatenate(parts, axis=0)
        gate_row = gate_ref[pl.ds(t, 1), :]
        gate_col = jnp.sum(jnp.where(eye, gate_row, 0.0), axis=1, keepdims=True)
        w = gate_col * _gelu(d)
        acc = jnp.zeros((tile, g.shape[1]), F32)
        for c in range(n_sel // tile):
            v = lax.bitcast_convert_type(cur[c * tile:(c + 1) * tile, :] & jnp.uint32(HI16), F32)
            acc = acc + v * w[c * tile:(c + 1) * tile, :]
        o_ref[pl.ds(t, 1), :] = jnp.sum(acc, axis=0, keepdims=True)

    def ring(i, carry):
        for s in range(PEER_RING):
            process(PEER_RING * i + s, s)
        return carry

    lax.fori_loop(0, tokens // PEER_RING, ring, 0)

    @pl.when(step == pl.num_programs(0) - 1)
    def _():
        for s in range(PEER_AHEAD):
            wait(s)


def _peer_experts(idx, gate, g, tab):
    m, d = g.shape
    n_sel = idx.shape[1]
    tb = _pick(m, 128, 8)
    nb = m // tb
    assert tb % PEER_RING == 0 and PEER_AHEAD <= 8
    blocks = idx.reshape(nb, tb, n_sel)
    idx_ext = jnp.concatenate([blocks, jnp.roll(blocks[:, :8], -1, axis=0)], axis=1)
    return pl.pallas_call(
        functools.partial(_peer_kernel, tokens=tb, n_sel=n_sel),
        grid=(nb,),
        in_specs=[pl.BlockSpec((None, tb + 8, n_sel), lambda i: (i, 0, 0), memory_space=pltpu.SMEM),
                  pl.BlockSpec((tb, d), lambda i: (i, 0)),
                  pl.BlockSpec((tb, n_sel), lambda i: (i, 0)),
                  pl.BlockSpec(memory_space=pl.ANY)],
        out_specs=pl.BlockSpec((tb, d), lambda i: (i, 0)),
        out_shape=jax.ShapeDtypeStruct((m, d), F32),
        scratch_shapes=[pltpu.VMEM((n_sel, d), jnp.uint32)] * PEER_RING
        + [pltpu.SemaphoreType.DMA((PEER_RING,))],
        compiler_params=_cparams(("arbitrary",)),
        name="peer_experts",
    )(idx_ext, g, gate, tab)


def _rope_tables(seq, rot_dim, pad):
    rows = seq // GRID_W
    row = jnp.repeat(jnp.arange(rows, dtype=F32), GRID_W)
    col = jnp.tile(jnp.arange(GRID_W, dtype=F32), rows)
    n_freq = rot_dim // 4
    inv = ROPE_BASE ** (-jnp.arange(n_freq, dtype=F32) / n_freq)
    ar, ac = row[:, None] * inv, col[:, None] * inv
    ang = jnp.concatenate([ar, ar, ac, ac], axis=-1)
    sign = jnp.concatenate([-jnp.ones(n_freq), jnp.ones(n_freq)] * 2).astype(F32)
    cos, sin = jnp.cos(ang), jnp.sin(ang) * sign
    reps = LANE // rot_dim
    cos, sin = jnp.tile(cos, (1, reps)), jnp.tile(sin, (1, reps))
    cos = jnp.concatenate([cos, jnp.ones((pad, LANE), F32)], axis=0)
    sin = jnp.concatenate([sin, jnp.zeros((pad, LANE), F32)], axis=0)
    return cos, sin


def kernel(x, c, ctx, c_ctx, w_mod, b_mod, norm1_gain, norm2_gain, w_in, a_v_gain, a_w_s, a_b_s,
           b_q_gain, b_kv_gain, b_w_uq, b_w_ukv, b_qn_gain, b_kn_gain, c_qn_gain, c_kn_gain, c_sink,
           w_out, peer_w_q, peer_subkeys, peer_u, peer_v):
    batch, seq, d = x.shape
    ctx_len = ctx.shape[1]
    depth = w_mod.shape[0]
    a_width = a_v_gain.shape[1]
    q_lora, kv_lora = b_q_gain.shape[1], b_kv_gain.shape[1]
    b_heads = b_w_uq.shape[2] // B_QK
    c_heads = c_sink.shape[1]
    c_kv = c_heads // 3
    p_a = 2 * a_width
    p_b = q_lora + kv_lora + QK_ROPE
    p_b_pad = -(-p_b // (2 * LANE)) * (2 * LANE)
    assert b_heads % 2 == 0 and (q_lora + kv_lora) % LANE == 0
    assert seq % WINDOW == 0 and ctx_len % WINDOW == 0

    rows_all = _Rows(batch, seq, ctx_len)
    bm_tab = rows_all.block(256)
    cos_b, sin_b = _rope_tables(seq, QK_ROPE, bm_tab)
    cos_c, sin_c = _rope_tables(seq, HEAD_DIM, bm_tab)

    c8 = jnp.zeros((8, d), F32).at[:batch].set(c).at[batch].set(c_ctx)
    xs = jnp.concatenate([x.reshape(batch * seq, d), ctx.reshape(batch * ctx_len, d)], axis=0)

    for layer in range(depth):
        need_ctx = layer < depth - 1
        mod3 = _modulation(c8, w_mod, b_mod[layer], layer).reshape(8, 1, N_MOD * d)

        w_l = w_in[layer]
        w_a = w_l[:, :p_a].astype(BF16)
        w_b = jnp.pad(w_l[:, p_a:p_a + p_b], ((0, 0), (0, p_b_pad - p_b))).astype(BF16)
        w_c = w_l[:, p_a + p_b:].astype(BF16)
        w_uq = b_w_uq[layer].reshape(q_lora, b_heads, B_QK)
        w_uq = jnp.concatenate([w_uq[:, :, :QK_NOPE].reshape(q_lora, -1),
                                w_uq[:, :, QK_NOPE:].reshape(q_lora, -1)], axis=1).astype(BF16)
        w_ukv = b_w_ukv[layer].reshape(kv_lora, b_heads, QK_NOPE + HEAD_DIM)
        w_ukv = jnp.concatenate([w_ukv[:, :, :QK_NOPE].reshape(kv_lora, -1),
                                 w_ukv[:, :, QK_NOPE:].reshape(kv_lora, -1)], axis=1).astype(BF16)

        (h,) = _norm_mod(xs, norm1_gain[layer], mod3, 0, 1, rows_all, [BF16])
        pa = _matmul(h, w_a)
        pb = _matmul(h, w_b)
        pc = _matmul(h, w_c, bn_target=1280)

        oa = _gmlp(pa, a_v_gain[layer], a_w_s[layer], a_b_s[layer])

        cq, ckv = _mla_pre(pb, b_q_gain[layer], b_kv_gain[layer], rows_all)
        q_raw = _matmul(cq, w_uq)
        kv_raw = _matmul(ckv, w_ukv)
        qh, kh, vh = _mla_post(q_raw, kv_raw, pb, (q_lora + kv_lora) // LANE, b_qn_gain[layer],
                               b_kn_gain[layer], cos_b, sin_b, rows_all, b_heads)
        ob = _mla_attention(qh, kh, vh, rows_all, b_heads, True)

        qc, kc, vc = _gqa_post(pc, c_qn_gain[layer], c_kn_gain[layer], cos_c, sin_c, rows_all,
                               c_heads, c_kv)
        oc = _gqa_attention(qc, kc, vc, c_sink[layer], rows_all, c_heads, c_kv, True)

        if need_ctx:
            rows = rows_all
            ob = jnp.concatenate([ob, _mla_attention(qh, kh, vh, rows_all, b_heads, False)], axis=0)
            oc = jnp.concatenate(
                [oc, _gqa_attention(qc, kc, vc, c_sink[layer], rows_all, c_heads, c_kv, False)], axis=0)
        else:
            rows = _Rows(batch, seq, 0)
        w_o = w_out[layer]
        n_b = b_heads * HEAD_DIM
        w_parts = [w_o[:a_width].astype(BF16), w_o[a_width:a_width + n_b].astype(BF16),
                   w_o[a_width + n_b:].astype(BF16)]
        xs = _matmul_residual([oa, ob, oc], w_parts, xs, mod3, 2, rows)

        g_bf, g = _norm_mod(xs, norm2_gain[layer], mod3, 3, 4, rows, [BF16, F32])
        pq = _matmul(g_bf, peer_w_q[layer].astype(BF16))
        idx, gate = _peer_route(pq, peer_subkeys[layer])
        n_sel = idx.shape[0] * idx.shape[1]
        idx = idx.reshape(n_sel, -1).T
        gate = gate.reshape(n_sel, -1).T
        y = _peer_experts(idx, gate, g, _pack_tables(peer_u, peer_v, layer))
        xs = _gated_add(xs, y, mod3, 5, rows)

    return xs[:batch * seq].reshape(batch, seq, d)
```

```python
import functools

import jax
import jax.numpy as jnp
from jax import lax
from jax.experimental import pallas as pl
from jax.experimental.pallas import tpu as pltpu

F32 = jnp.float32
BF16 = jnp.bfloat16

LANE = 128
HEAD_DIM = 128
QK_NOPE = 128
QK_ROPE = 64
B_QK = QK_NOPE + QK_ROPE
B_QK_PAD = 2 * LANE
GRID_W = 64
WINDOW = 128
CHUNK = 128
PEER_TOPK = 16
N_MOD = 6
ROPE_BASE = 10000.0
EPS = 1e-6
NEG = -1e30
LOG2E = 1.4426950408889634
MIB = 1024 * 1024
VMEM_LIMIT = 56 * MIB
ROW_BLOCK = 512


def _cparams(semantics, vmem=VMEM_LIMIT):
    return pltpu.CompilerParams(dimension_semantics=semantics, vmem_limit_bytes=vmem)


def _pick(n, target, align=LANE):
    if n <= target:
        return n
    best = None
    d = align
    while d <= target:
        if n % d == 0:
            best = d
        d += align
    assert best is not None, (n, target, align)
    return best


def _gelu(x):
    return 0.5 * x * (1.0 + jnp.tanh(0.7978845608028654 * (x + 0.044715 * (x * x * x))))


def _dot_nt(a, b):
    return lax.dot_general(a, b, (((1,), (1,)), ((), ())), preferred_element_type=F32)


def _modulation_kernel(c_ref, w_ref, b_ref, o_ref, acc_ref):
    k = pl.program_id(1)

    @pl.when(k == 0)
    def _():
        acc_ref[...] = jnp.zeros_like(acc_ref)

    c = c_ref[...]
    silu = c / (1.0 + jnp.exp(-c))
    acc_ref[...] += jnp.dot(silu.astype(BF16), w_ref[...].astype(BF16), preferred_element_type=F32)

    @pl.when(k == pl.num_programs(1) - 1)
    def _():
        o_ref[...] = acc_ref[...] + b_ref[...]


def _modulation(c8, w_all, b, layer):
    rows, d = c8.shape
    n = w_all.shape[2]
    bn = _pick(n, 2048)
    bk = _pick(d, 1024)
    return pl.pallas_call(
        _modulation_kernel,
        grid=(n // bn, d // bk),
        in_specs=[pl.BlockSpec((rows, bk), lambda j, k: (0, k)),
                  pl.BlockSpec((None, bk, bn), lambda j, k: (layer, k, j)),
                  pl.BlockSpec((1, bn), lambda j, k: (0, j))],
        out_specs=pl.BlockSpec((rows, bn), lambda j, k: (0, j)),
        out_shape=jax.ShapeDtypeStruct((rows, n), F32),
        scratch_shapes=[pltpu.VMEM((rows, bn), F32)],
        compiler_params=_cparams(("parallel", "arbitrary")),
        name="modulation",
    )(c8, w_all, b.reshape(1, n))


class _Rows:
    def __init__(self, batch, seq, ctx_len):
        self.batch, self.seq, self.ctx_len = batch, seq, ctx_len
        self.m_lat = batch * seq
        self.m_ctx = batch * ctx_len
        self.m = self.m_lat + self.m_ctx

    def block(self, target):
        bm = target
        while self.seq % bm or self.m_ctx % bm:
            bm //= 2
        assert bm >= 8
        return bm

    def mod_row(self, i, bm):
        return jnp.where(i < self.m_lat // bm, (i * bm) // self.seq, self.batch)


def _norm_mod_kernel(x_ref, gain_ref, shift_ref, scale_ref, *out_refs):
    x = x_ref[...]
    y = x * lax.rsqrt(jnp.mean(x * x, axis=-1, keepdims=True) + EPS) * gain_ref[...]
    h = y * (1.0 + scale_ref[0]) + shift_ref[0]
    for o in out_refs:
        o[...] = h.astype(o.dtype)


def _norm_mod(x, gain, mod3, shift_chunk, scale_chunk, rows, out_dtypes):
    m, d = x.shape
    bm = rows.block(ROW_BLOCK)
    outs = pl.pallas_call(
        _norm_mod_kernel,
        grid=(m // bm,),
        in_specs=[pl.BlockSpec((bm, d), lambda i: (i, 0)),
                  pl.BlockSpec((1, d), lambda i: (0, 0)),
                  pl.BlockSpec((1, 1, d), lambda i: (rows.mod_row(i, bm), 0, shift_chunk)),
                  pl.BlockSpec((1, 1, d), lambda i: (rows.mod_row(i, bm), 0, scale_chunk))],
        out_specs=[pl.BlockSpec((bm, d), lambda i: (i, 0)) for _ in out_dtypes],
        out_shape=[jax.ShapeDtypeStruct((m, d), dt) for dt in out_dtypes],
        compiler_params=_cparams(("parallel",)),
        name="norm_mod",
    )(x, gain.reshape(1, d), mod3, mod3)
    return outs


def _mm_kernel(a_ref, w_ref, o_ref):
    o_ref[...] = jnp.dot(a_ref[...], w_ref[...], preferred_element_type=F32).astype(o_ref.dtype)


def _mm_res_kernel(*refs, parts):
    a_refs, w_refs = refs[:parts], refs[parts:2 * parts]
    res_ref, gate_ref, o_ref = refs[2 * parts:]
    acc = jnp.dot(a_refs[0][...], w_refs[0][...], preferred_element_type=F32)
    for a_ref, w_ref in zip(a_refs[1:], w_refs[1:]):
        acc = acc + jnp.dot(a_ref[...], w_ref[...], preferred_element_type=F32)
    o_ref[...] = res_ref[...] + gate_ref[0] * acc


def _matmul(a, w, out_dtype=F32, bm_target=1024, bn_target=1024):
    m, k = a.shape
    n = w.shape[1]
    bm = _pick(m, bm_target, 8)
    bn = _pick(n, bn_target)
    return pl.pallas_call(
        _mm_kernel,
        grid=(m // bm, n // bn),
        in_specs=[pl.BlockSpec((bm, k), lambda i, j: (i, 0)),
                  pl.BlockSpec((k, bn), lambda i, j: (0, j))],
        out_specs=pl.BlockSpec((bm, bn), lambda i, j: (i, j)),
        out_shape=jax.ShapeDtypeStruct((m, n), out_dtype),
        compiler_params=_cparams(("parallel", "parallel")),
        name="matmul",
    )(a, w)


def _gated_add_kernel(res_ref, y_ref, gate_ref, o_ref):
    o_ref[...] = res_ref[...] + gate_ref[0] * y_ref[...]


def _gated_add(res, y, mod3, gate_chunk, rows):
    m, d = y.shape
    bm = rows.block(ROW_BLOCK)
    return pl.pallas_call(
        _gated_add_kernel,
        grid=(m // bm,),
        in_specs=[pl.BlockSpec((bm, d), lambda i: (i, 0)),
                  pl.BlockSpec((bm, d), lambda i: (i, 0)),
                  pl.BlockSpec((1, 1, d), lambda i: (rows.mod_row(i, bm), 0, gate_chunk))],
        out_specs=pl.BlockSpec((bm, d), lambda i: (i, 0)),
        out_shape=jax.ShapeDtypeStruct((m, d), F32),
        compiler_params=_cparams(("parallel",)),
        name="gated_add",
    )(res, y, mod3)


def _matmul_residual(a_parts, w_parts, res, mod3, gate_chunk, rows):
    m, n = rows.m, res.shape[1]
    bm = rows.block(1024)
    bn = _pick(n, 512)
    parts = len(a_parts)
    a_specs = [pl.BlockSpec((bm, a.shape[1]), lambda i, j: (i, 0)) for a in a_parts]
    w_specs = [pl.BlockSpec((w.shape[0], bn), lambda i, j: (0, j)) for w in w_parts]
    return pl.pallas_call(
        functools.partial(_mm_res_kernel, parts=parts),
        grid=(m // bm, n // bn),
        in_specs=a_specs + w_specs + [
            pl.BlockSpec((bm, bn), lambda i, j: (i, j)),
            pl.BlockSpec((1, 1, bn), lambda i, j: (rows.mod_row(i, bm), 0, gate_chunk * (n // bn) + j))],
        out_specs=pl.BlockSpec((bm, bn), lambda i, j: (i, j)),
        out_shape=jax.ShapeDtypeStruct((m, n), F32),
        compiler_params=_cparams(("parallel", "parallel")),
        name="matmul_residual",
    )(*a_parts, *w_parts, res, mod3)


def _gmlp_kernel(p_ref, gain_ref, ws_ref, bs_ref, o_ref, *, width, chunks):
    ws = ws_ref[...].astype(BF16)
    bs = bs_ref[...]
    for c in range(chunks):
        r0 = c * CHUNK
        z = _gelu(p_ref[r0:r0 + CHUNK, :])
        u = z[:, :width]
        parts = []
        for h in range(width // HEAD_DIM):
            vh = z[:, width + h * HEAD_DIM: width + (h + 1) * HEAD_DIM]
            vh = vh * lax.rsqrt(jnp.mean(vh * vh, axis=-1, keepdims=True) + EPS)
            parts.append((vh * gain_ref[:, h * HEAD_DIM:(h + 1) * HEAD_DIM]).astype(BF16))
        vn = jnp.concatenate(parts, axis=1)
        s = jnp.dot(ws, vn, preferred_element_type=F32) + bs
        o_ref[r0:r0 + CHUNK, :] = (u * s).astype(o_ref.dtype)


def _gmlp(pa, v_gain, w_s, b_s):
    m, two_w = pa.shape
    width = two_w // 2
    chunks = 2
    bm = chunks * CHUNK
    return pl.pallas_call(
        functools.partial(_gmlp_kernel, width=width, chunks=chunks),
        grid=(m // bm,),
        in_specs=[pl.BlockSpec((bm, two_w), lambda i: (i, 0)),
                  pl.BlockSpec((1, width), lambda i: (0, 0)),
                  pl.BlockSpec((CHUNK, CHUNK), lambda i: (0, 0)),
                  pl.BlockSpec((CHUNK, 1), lambda i: (0, 0))],
        out_specs=pl.BlockSpec((bm, width), lambda i: (i, 0)),
        out_shape=jax.ShapeDtypeStruct((m, width), BF16),
        compiler_params=_cparams(("parallel",)),
        name="gmlp",
    )(pa, v_gain.reshape(1, width), w_s, b_s.reshape(CHUNK, 1))


def _rope(x, cos, sin_signed, half):
    lane = lax.broadcasted_iota(jnp.int32, x.shape, 1)
    first = (lane & (2 * half - 1)) < half
    partner = jnp.where(first, pltpu.roll(x, LANE - half, axis=1), pltpu.roll(x, half, axis=1))
    return x * cos + partner * sin_signed


def _mla_pre_kernel(p_ref, qg_ref, kvg_ref, cq_ref, ckv_ref, *, q_lora, kv_lora):
    cq = p_ref[:, :q_lora]
    cq_ref[...] = (cq * lax.rsqrt(jnp.mean(cq * cq, axis=-1, keepdims=True) + EPS)
                   * qg_ref[...]).astype(cq_ref.dtype)
    ckv = p_ref[:, q_lora:q_lora + kv_lora]
    ckv_ref[...] = (ckv * lax.rsqrt(jnp.mean(ckv * ckv, axis=-1, keepdims=True) + EPS)
                    * kvg_ref[...]).astype(ckv_ref.dtype)


def _mla_pre(pb, q_gain, kv_gain, rows):
    m, n = pb.shape
    q_lora, kv_lora = q_gain.shape[0], kv_gain.shape[0]
    bm = rows.block(ROW_BLOCK)
    return pl.pallas_call(
        functools.partial(_mla_pre_kernel, q_lora=q_lora, kv_lora=kv_lora),
        grid=(m // bm,),
        in_specs=[pl.BlockSpec((bm, n), lambda i: (i, 0)),
                  pl.BlockSpec((1, q_lora), lambda i: (0, 0)),
                  pl.BlockSpec((1, kv_lora), lambda i: (0, 0))],
        out_specs=[pl.BlockSpec((bm, q_lora), lambda i: (i, 0)),
                   pl.BlockSpec((bm, kv_lora), lambda i: (i, 0))],
        out_shape=[jax.ShapeDtypeStruct((m, q_lora), BF16),
                   jax.ShapeDtypeStruct((m, kv_lora), BF16)],
        compiler_params=_cparams(("parallel",)),
        name="mla_pre",
    )(pb, q_gain.reshape(1, q_lora), kv_gain.reshape(1, kv_lora))


def _mla_post_kernel(q_ref, kv_ref, kr_ref, qgn_ref, qgr_ref, kgn_ref, kgr_ref, cos_ref, sin_ref,
                     qh_ref, kh_ref, vh_ref, *, heads):
    cos, sin = cos_ref[...], sin_ref[...]
    lane = lax.broadcasted_iota(jnp.int32, cos.shape, 1)
    low = lane < QK_ROPE
    half = QK_ROPE // 4
    zeros = jnp.zeros(cos.shape, F32)

    kr = jnp.where(low, kr_ref[...], 0.0)
    kr_ss = jnp.sum(kr * kr, axis=-1, keepdims=True)
    kr_rot = _rope(kr * kgr_ref[...], cos, sin, half)

    for hp in range(heads // 2):
        qr = q_ref[:, heads * QK_NOPE + hp * LANE: heads * QK_NOPE + (hp + 1) * LANE]
        qr2 = qr * qr
        ss_lo = jnp.sum(jnp.where(low, qr2, 0.0), axis=-1, keepdims=True)
        ss_hi = jnp.sum(jnp.where(low, 0.0, qr2), axis=-1, keepdims=True)
        rinv = []
        for j, ss_r in enumerate((ss_lo, ss_hi)):
            h = 2 * hp + j
            qn = q_ref[:, h * QK_NOPE:(h + 1) * QK_NOPE]
            r = lax.rsqrt((jnp.sum(qn * qn, axis=-1, keepdims=True) + ss_r) * (1.0 / B_QK) + EPS)
            rinv.append(r)
            qh_ref[h, :, :QK_NOPE] = (qn * r * qgn_ref[...]).astype(qh_ref.dtype)
        qrot = _rope(qr * jnp.where(low, rinv[0], rinv[1]) * qgr_ref[...], cos, sin, half)
        qh_ref[2 * hp, :, QK_NOPE:] = jnp.where(low, qrot, zeros).astype(qh_ref.dtype)
        qh_ref[2 * hp + 1, :, QK_NOPE:] = jnp.where(
            low, pltpu.roll(qrot, QK_ROPE, axis=1), zeros).astype(qh_ref.dtype)

    for h in range(heads):
        kn = kv_ref[:, h * QK_NOPE:(h + 1) * QK_NOPE]
        r = lax.rsqrt((jnp.sum(kn * kn, axis=-1, keepdims=True) + kr_ss) * (1.0 / B_QK) + EPS)
        kh_ref[h, :, :QK_NOPE] = (kn * r * kgn_ref[...]).astype(kh_ref.dtype)
        kh_ref[h, :, QK_NOPE:] = (kr_rot * r).astype(kh_ref.dtype)
        vh_ref[h] = kv_ref[:, (heads + h) * HEAD_DIM:(heads + h + 1) * HEAD_DIM].astype(vh_ref.dtype)


def _mla_post(q_raw, kv_raw, pb, kr_block, qn_gain, kn_gain, cos_t, sin_t, rows, heads):
    m = q_raw.shape[0]
    bm = rows.block(ROW_BLOCK)
    n_lat, n_tab = rows.m_lat // bm, rows.seq // bm
    tab = lambda i: (jnp.where(i < n_lat, i % n_tab, n_tab), 0)
    pair = lambda g: jnp.concatenate([g, g]).reshape(1, LANE)
    return pl.pallas_call(
        functools.partial(_mla_post_kernel, heads=heads),
        grid=(m // bm,),
        in_specs=[pl.BlockSpec((bm, q_raw.shape[1]), lambda i: (i, 0)),
                  pl.BlockSpec((bm, kv_raw.shape[1]), lambda i: (i, 0)),
                  pl.BlockSpec((bm, LANE), lambda i: (i, kr_block)),
                  pl.BlockSpec((1, QK_NOPE), lambda i: (0, 0)),
                  pl.BlockSpec((1, LANE), lambda i: (0, 0)),
                  pl.BlockSpec((1, QK_NOPE), lambda i: (0, 0)),
                  pl.BlockSpec((1, LANE), lambda i: (0, 0)),
                  pl.BlockSpec((bm, LANE), tab),
                  pl.BlockSpec((bm, LANE), tab)],
        out_specs=[pl.BlockSpec((heads, bm, B_QK_PAD), lambda i: (0, i, 0)),
                   pl.BlockSpec((heads, bm, B_QK_PAD), lambda i: (0, i, 0)),
                   pl.BlockSpec((heads, bm, HEAD_DIM), lambda i: (0, i, 0))],
        out_shape=[jax.ShapeDtypeStruct((heads, m, B_QK_PAD), BF16),
                   jax.ShapeDtypeStruct((heads, m, B_QK_PAD), BF16),
                   jax.ShapeDtypeStruct((heads, m, HEAD_DIM), BF16)],
        compiler_params=_cparams(("parallel",)),
        name="mla_post",
    )(q_raw, kv_raw, pb, qn_gain[:QK_NOPE].reshape(1, QK_NOPE), pair(qn_gain[QK_NOPE:]),
      kn_gain[:QK_NOPE].reshape(1, QK_NOPE), pair(kn_gain[QK_NOPE:]), cos_t, sin_t)


def _mla_attn_kernel(q_ref, kc_ref, vc_ref, *rest, with_lat):
    if with_lat:
        kl_ref, vl_ref, o_ref = rest
    else:
        (o_ref,) = rest
    c = (B_QK ** -0.5) * LOG2E
    tn = (((0,), (0,)), ((), ()))
    q = q_ref[...]
    s_c = _dot_nt(kc_ref[...], q)
    m = jnp.max(s_c, axis=0, keepdims=True)
    if with_lat:
        s_l = _dot_nt(kl_ref[...], q)
        m = jnp.maximum(m, jnp.max(s_l, axis=0, keepdims=True))
    p_c = jnp.exp2((s_c - m) * c)
    den = jnp.sum(p_c, axis=0, keepdims=True)
    acc = lax.dot_general(vc_ref[...], p_c.astype(BF16), tn, preferred_element_type=F32)
    if with_lat:
        p_l = jnp.exp2((s_l - m) * c)
        den = den + jnp.sum(p_l, axis=0, keepdims=True)
        acc = acc + lax.dot_general(vl_ref[...], p_l.astype(BF16), tn, preferred_element_type=F32)
    o_ref[...] = (acc / den).T.astype(o_ref.dtype)


def _mla_attention(qh, kh, vh, rows, heads, latent_queries):
    b, s, c = rows.batch, rows.seq, rows.ctx_len
    ctx_blk0 = rows.m_lat // c
    if latent_queries:
        tq = _pick(s, 1024, 8)
        nq = s // tq
        q_map = lambda bi, h, i: (h, bi * nq + i, 0)
        o_map = lambda bi, h, i: (bi * nq + i, h)
        m_out = rows.m_lat
    else:
        tq, nq = c, 1
        q_map = lambda bi, h, i: (h, ctx_blk0 + bi, 0)
        o_map = lambda bi, h, i: (bi, h)
        m_out = rows.m_ctx
    in_specs = [pl.BlockSpec((None, tq, B_QK_PAD), q_map),
                pl.BlockSpec((None, c, B_QK_PAD), lambda bi, h, i: (h, ctx_blk0 + bi, 0)),
                pl.BlockSpec((None, c, HEAD_DIM), lambda bi, h, i: (h, ctx_blk0 + bi, 0))]
    args = [qh, kh, vh]
    if latent_queries:
        in_specs += [pl.BlockSpec((None, s, B_QK_PAD), lambda bi, h, i: (h, bi, 0)),
                     pl.BlockSpec((None, s, HEAD_DIM), lambda bi, h, i: (h, bi, 0))]
        args += [kh, vh]
    return pl.pallas_call(
        functools.partial(_mla_attn_kernel, with_lat=latent_queries),
        grid=(b, heads, nq),
        in_specs=in_specs,
        out_specs=pl.BlockSpec((tq, HEAD_DIM), o_map),
        out_shape=jax.ShapeDtypeStruct((m_out, heads * HEAD_DIM), BF16),
        compiler_params=_cparams(("parallel", "parallel", "parallel")),
        name="mla_attention",
    )(*args)


def _gqa_post_kernel(p_ref, qg_ref, kg_ref, cos_ref, sin_ref, q_ref, k_ref, v_ref, *, q_heads, kv_heads):
    cos, sin = cos_ref[...], sin_ref[...]
    half = HEAD_DIM // 4

    def norm_rope(x, gain):
        xn = x * lax.rsqrt(jnp.mean(x * x, axis=-1, keepdims=True) + EPS) * gain
        return _rope(xn, cos, sin, half)

    for h in range(q_heads):
        sl = slice(h * HEAD_DIM, (h + 1) * HEAD_DIM)
        q_ref[:, sl] = norm_rope(p_ref[:, sl], qg_ref[...]).astype(q_ref.dtype)
    for h in range(kv_heads):
        sl = slice(h * HEAD_DIM, (h + 1) * HEAD_DIM)
        k0 = q_heads * HEAD_DIM
        v0 = (q_heads + kv_heads) * HEAD_DIM
        k_ref[:, sl] = norm_rope(p_ref[:, k0 + h * HEAD_DIM:k0 + (h + 1) * HEAD_DIM],
                                 kg_ref[...]).astype(k_ref.dtype)
        v_ref[:, sl] = p_ref[:, v0 + h * HEAD_DIM:v0 + (h + 1) * HEAD_DIM].astype(v_ref.dtype)


def _gqa_post(pc, qn_gain, kn_gain, cos_t, sin_t, rows, q_heads, kv_heads):
    m, n = pc.shape
    bm = rows.block(ROW_BLOCK)
    n_lat, n_tab = rows.m_lat // bm, rows.seq // bm
    tab = lambda i: (jnp.where(i < n_lat, i % n_tab, n_tab), 0)
    return pl.pallas_call(
        functools.partial(_gqa_post_kernel, q_heads=q_heads, kv_heads=kv_heads),
        grid=(m // bm,),
        in_specs=[pl.BlockSpec((bm, n), lambda i: (i, 0)),
                  pl.BlockSpec((1, HEAD_DIM), lambda i: (0, 0)),
                  pl.BlockSpec((1, HEAD_DIM), lambda i: (0, 0)),
                  pl.BlockSpec((bm, LANE), tab),
                  pl.BlockSpec((bm, LANE), tab)],
        out_specs=[pl.BlockSpec((bm, q_heads * HEAD_DIM), lambda i: (i, 0)),
                   pl.BlockSpec((bm, kv_heads * HEAD_DIM), lambda i: (i, 0)),
                   pl.BlockSpec((bm, kv_heads * HEAD_DIM), lambda i: (i, 0))],
        out_shape=[jax.ShapeDtypeStruct((m, q_heads * HEAD_DIM), BF16),
                   jax.ShapeDtypeStruct((m, kv_heads * HEAD_DIM), BF16),
                   jax.ShapeDtypeStruct((m, kv_heads * HEAD_DIM), BF16)],
        compiler_params=_cparams(("parallel",)),
        name="gqa_post",
    )(pc, qn_gain.reshape(1, HEAD_DIM), kn_gain.reshape(1, HEAD_DIM), cos_t, sin_t)


def _gqa_attn_kernel(sink_ref, q_ref, kc_ref, vc_ref, *rest, group, seq, banded):
    if banded:
        kp_ref, kq_ref, kn_ref, vp_ref, vq_ref, vn_ref, o_ref = rest
    else:
        (o_ref,) = rest
    scale = HEAD_DIM ** -0.5
    blk = pl.program_id(1)
    rows_q = group * WINDOW
    kv_heads = kc_ref.shape[1] // HEAD_DIM
    row = lax.broadcasted_iota(jnp.int32, (rows_q, 1), 0)

    for kvh in range(kv_heads):
        hs = slice(kvh * HEAD_DIM, (kvh + 1) * HEAD_DIM)
        q0 = kvh * group * HEAD_DIM
        q = jnp.concatenate([q_ref[:, q0 + g * HEAD_DIM:q0 + (g + 1) * HEAD_DIM]
                             for g in range(group)], axis=0)
        sink = jnp.zeros((rows_q, 1), F32)
        for g in range(group):
            in_g = (row >= g * WINDOW) & (row < (g + 1) * WINDOW)
            sink = jnp.where(in_g, sink_ref[kvh * group + g], sink)

        s_c = _dot_nt(q, kc_ref[:, hs]) * scale
        m = jnp.maximum(jnp.max(s_c, axis=-1, keepdims=True), sink)
        if banded:
            kb = jnp.concatenate([kp_ref[:, hs], kq_ref[:, hs], kn_ref[:, hs]], axis=0)
            vb = jnp.concatenate([vp_ref[:, hs], vq_ref[:, hs], vn_ref[:, hs]], axis=0)
            s_b = _dot_nt(q, kb) * scale
            qpos = lax.broadcasted_iota(jnp.int32, s_b.shape, 0) & (WINDOW - 1)
            krel = lax.broadcasted_iota(jnp.int32, s_b.shape, 1) - WINDOW
            kpos = blk * WINDOW + krel
            valid = (jnp.abs(qpos - krel) <= WINDOW) & (kpos >= 0) & (kpos < seq)
            s_b = jnp.where(valid, s_b, NEG)
            m = jnp.maximum(m, jnp.max(s_b, axis=-1, keepdims=True))
        p_c = jnp.exp(s_c - m)
        den = jnp.sum(p_c, axis=-1, keepdims=True) + jnp.exp(sink - m)
        acc = jnp.dot(p_c.astype(BF16), vc_ref[:, hs], preferred_element_type=F32)
        if banded:
            p_b = jnp.exp(s_b - m)
            den = den + jnp.sum(p_b, axis=-1, keepdims=True)
            acc = acc + jnp.dot(p_b.astype(BF16), vb, preferred_element_type=F32)
        o = acc / den
        for g in range(group):
            o_ref[:, q0 + g * HEAD_DIM:q0 + (g + 1) * HEAD_DIM] = (
                o[g * WINDOW:(g + 1) * WINDOW].astype(o_ref.dtype))


def _gqa_attention(qc, kc, vc, sink, rows, q_heads, kv_heads, latent_queries):
    b, s, c = rows.batch, rows.seq, rows.ctx_len
    group = q_heads // kv_heads
    qw, kw = q_heads * HEAD_DIM, kv_heads * HEAD_DIM
    ctx_blk0 = rows.m_lat // c
    ctx_spec = lambda: pl.BlockSpec((c, kw), lambda bi, n: (ctx_blk0 + bi, 0))
    if latent_queries:
        nb = s // WINDOW
        qrow = lambda bi, n: bi * nb + n
        m_out = rows.m_lat
        orow = qrow
    else:
        nb = c // WINDOW
        qrow = lambda bi, n: rows.m_lat // WINDOW + bi * nb + n
        orow = lambda bi, n: bi * nb + n
        m_out = rows.m_ctx
    in_specs = [pl.BlockSpec(memory_space=pltpu.SMEM),
                pl.BlockSpec((WINDOW, qw), lambda bi, n: (qrow(bi, n), 0)),
                ctx_spec(), ctx_spec()]
    args = [sink, qc, kc, vc]
    if latent_queries:
        band = [lambda bi, n: (bi * nb + jnp.maximum(n - 1, 0), 0),
                lambda bi, n: (bi * nb + n, 0),
                lambda bi, n: (bi * nb + jnp.minimum(n + 1, nb - 1), 0)]
        in_specs += [pl.BlockSpec((WINDOW, kw), f) for f in band] * 2
        args += [kc, kc, kc, vc, vc, vc]
    return pl.pallas_call(
        functools.partial(_gqa_attn_kernel, group=group, seq=s, banded=latent_queries),
        grid=(b, nb),
        in_specs=in_specs,
        out_specs=pl.BlockSpec((WINDOW, qw), lambda bi, n: (orow(bi, n), 0)),
        out_shape=jax.ShapeDtypeStruct((m_out, qw), BF16),
        compiler_params=_cparams(("parallel", "parallel")),
        name="gqa_attention",
    )(*args)


def _extract_topk(s, pos, ids, count, val_ref, id_ref):
    beyond = 3.0e38
    for r in range(count):
        top = jnp.max(s, axis=0, keepdims=True)
        first = jnp.min(jnp.where(s == top, pos, beyond), axis=0, keepdims=True)
        hit = pos == first
        val_ref[r:r + 1, :] = top
        if ids is None:
            id_ref[r:r + 1, :] = first
        else:
            id_ref[r:r + 1, :] = jnp.max(jnp.where(hit, ids, -1.0), axis=0, keepdims=True)
        s = jnp.where(hit, -jnp.inf, s)


_PAIR_SPANS = [(a, PEER_TOPK // (a + 1)) for a in range(PEER_TOPK)]
_N_PAIRS = sum(n for _, n in _PAIR_SPANS)
_N_PAIRS_PAD = -(-_N_PAIRS // 8) * 8


def _route_kernel(q_ref, sk_ref, idx_ref, gate_ref, tv_ref, ti_ref, cs_ref, cp_ref, ci_ref, bs_ref,
                  bi_ref, *, n_keys):
    tt = q_ref.shape[0]
    key_id = lax.broadcasted_iota(jnp.int32, (n_keys, tt), 0).astype(F32)
    for p in range(2):
        qp = q_ref[:, p * LANE:(p + 1) * LANE].astype(BF16)
        scores = _dot_nt(sk_ref[p], qp)
        _extract_topk(scores, key_id, None, PEER_TOPK, tv_ref.at[p], ti_ref.at[p])
    cs_ref[...] = jnp.full(cs_ref.shape, -jnp.inf, F32)
    cp_ref[...] = jnp.full(cp_ref.shape, 1.0e6, F32)
    ci_ref[...] = jnp.full(ci_ref.shape, -1.0, F32)
    off = 0
    for a, n in _PAIR_SPANS:
        rs = slice(off, off + n)
        cs_ref[rs, :] = tv_ref[0, a:a + 1, :] + tv_ref[1, 0:n, :]
        cp_ref[rs, :] = float(a * PEER_TOPK) + lax.broadcasted_iota(jnp.int32, (n, tt), 0).astype(F32)
        ci_ref[rs, :] = ti_ref[0, a:a + 1, :] * float(n_keys) + ti_ref[1, 0:n, :]
        off += n
    _extract_topk(cs_ref[...], cp_ref[...], ci_ref[...], PEER_TOPK, bs_ref, bi_ref)
    best = bs_ref[...]
    e = jnp.exp(best - best[0:1, :])
    gate_ref[...] = e / jnp.sum(e, axis=0, keepdims=True)
    idx_ref[...] = bi_ref[...].astype(jnp.int32)


def _peer_route(pq, subkeys):
    m = pq.shape[0]
    heads, _, n_keys, half = subkeys.shape
    assert half == LANE
    tt = _pick(m, 1024)
    k = PEER_TOPK
    return pl.pallas_call(
        functools.partial(_route_kernel, n_keys=n_keys),
        grid=(m // tt, heads),
        in_specs=[pl.BlockSpec((tt, 2 * LANE), lambda i, h: (i, h)),
                  pl.BlockSpec((None, 2, n_keys, half), lambda i, h: (h, 0, 0, 0))],
        out_specs=[pl.BlockSpec((None, k, tt), lambda i, h: (h, 0, i)),
                   pl.BlockSpec((None, k, tt), lambda i, h: (h, 0, i))],
        out_shape=[jax.ShapeDtypeStruct((heads, k, m), jnp.int32),
                   jax.ShapeDtypeStruct((heads, k, m), F32)],
        scratch_shapes=[pltpu.VMEM((2, k, tt), F32), pltpu.VMEM((2, k, tt), F32)]
        + [pltpu.VMEM((_N_PAIRS_PAD, tt), F32)] * 3
        + [pltpu.VMEM((k, tt), F32), pltpu.VMEM((k, tt), F32)],
        compiler_params=_cparams(("parallel", "parallel")),
        name="peer_route",
    )(pq, subkeys.astype(BF16))


HI16 = 0xFFFF0000


def _pack_kernel(u_ref, v_ref, o_ref):
    ub = lax.bitcast_convert_type(u_ref[...].astype(BF16).astype(F32), jnp.uint32)
    vb = lax.bitcast_convert_type(v_ref[...].astype(BF16).astype(F32), jnp.uint32)
    o_ref[...] = (vb & jnp.uint32(HI16)) | (ub >> 16)


def _pack_tables(u_all, v_all, layer):
    _, e, d = u_all.shape
    be = _pick(e, ROW_BLOCK, 8)
    spec = pl.BlockSpec((None, be, d), lambda i: (layer, i, 0))
    return pl.pallas_call(
        _pack_kernel,
        grid=(e // be,),
        in_specs=[spec, spec],
        out_specs=pl.BlockSpec((be, d), lambda i: (i, 0)),
        out_shape=jax.ShapeDtypeStruct((e, d), jnp.uint32),
        compiler_params=_cparams(("parallel",)),
        name="peer_pack",
    )(u_all, v_all)


PEER_RING = 8
PEER_AHEAD = PEER_RING - 1


def _peer_kernel(idx_ref, g_ref, gate_ref, tab_hbm, o_ref, *scratch, tokens, n_sel):
    bufs, sem = scratch[:PEER_RING], scratch[PEER_RING]
    step = pl.program_id(0)
    tile = 8

    def row_copy(e, s, k):
        return pltpu.make_async_copy(tab_hbm.at[pl.ds(e, 1), :], bufs[s].at[pl.ds(k, 1), :], sem.at[s])

    def wait(s):
        pltpu.make_async_copy(tab_hbm.at[pl.ds(0, n_sel), :], bufs[s], sem.at[s]).wait()

    @pl.when(step == 0)
    def _():
        for t in range(PEER_AHEAD):
            for k in range(n_sel):
                row_copy(idx_ref[t, k], t, k).start()

    eye = (lax.broadcasted_iota(jnp.int32, (n_sel, n_sel), 0)
           == lax.broadcasted_iota(jnp.int32, (n_sel, n_sel), 1))

    def process(t, s):
        cur, s_nxt = bufs[s], (s + PEER_AHEAD) % PEER_RING
        wait(s)
        g = g_ref[pl.ds(t, 1), :]
        parts = []
        for c in range(n_sel // tile):
            for k in range(c * tile, (c + 1) * tile):
                row_copy(idx_ref[t + PEER_AHEAD, k], s_nxt, k).start(priority=k % 2)
            u = lax.bitcast_convert_type(cur[c * tile:(c + 1) * tile, :] << 16, F32)
            parts.append(jnp.sum(u * g, axis=1, keepdims=True))
        d = jnp.concatenate(parts, axis=0)
        gate_row = gate_ref[pl.ds(t, 1), :]
        gate_col = jnp.sum(jnp.where(eye, gate_row, 0.0), axis=1, keepdims=True)
        w = gate_col * _gelu(d)
        acc = jnp.zeros((tile, g.shape[1]), F32)
        for c in range(n_sel // tile):
            v = lax.bitcast_convert_type(cur[c * tile:(c + 1) * tile, :] & jnp.uint32(HI16), F32)
            acc = acc + v * w[c * tile:(c + 1) * tile, :]
        o_ref[pl.ds(t, 1), :] = jnp.sum(acc, axis=0, keepdims=True)

    def ring(i, carry):
        for s in range(PEER_RING):
            process(PEER_RING * i + s, s)
        return carry

    lax.fori_loop(0, tokens // PEER_RING, ring, 0)

    @pl.when(step == pl.num_programs(0) - 1)
    def _():
        for s in range(PEER_AHEAD):
            wait(s)


def _peer_experts(idx, gate, g, tab):
    m, d = g.shape
    n_sel = idx.shape[1]
    tb = _pick(m, 128, 8)
    nb = m // tb
    assert tb % PEER_RING == 0 and PEER_AHEAD <= 8
    blocks = idx.reshape(nb, tb, n_sel)
    idx_ext = jnp.concatenate([blocks, jnp.roll(blocks[:, :8], -1, axis=0)], axis=1)
    return pl.pallas_call(
        functools.partial(_peer_kernel, tokens=tb, n_sel=n_sel),
        grid=(nb,),
        in_specs=[pl.BlockSpec((None, tb + 8, n_sel), lambda i: (i, 0, 0), memory_space=pltpu.SMEM),
                  pl.BlockSpec((tb, d), lambda i: (i, 0)),
                  pl.BlockSpec((tb, n_sel), lambda i: (i, 0)),
                  pl.BlockSpec(memory_space=pl.ANY)],
        out_specs=pl.BlockSpec((tb, d), lambda i: (i, 0)),
        out_shape=jax.ShapeDtypeStruct((m, d), F32),
        scratch_shapes=[pltpu.VMEM((n_sel, d), jnp.uint32)] * PEER_RING
        + [pltpu.SemaphoreType.DMA((PEER_RING,))],
        compiler_params=_cparams(("arbitrary",)),
        name="peer_experts",
    )(idx_ext, g, gate, tab)


def _rope_tables(seq, rot_dim, pad):
    rows = seq // GRID_W
    row = jnp.repeat(jnp.arange(rows, dtype=F32), GRID_W)
    col = jnp.tile(jnp.arange(GRID_W, dtype=F32), rows)
    n_freq = rot_dim // 4
    inv = ROPE_BASE ** (-jnp.arange(n_freq, dtype=F32) / n_freq)
    ar, ac = row[:, None] * inv, col[:, None] * inv
    ang = jnp.concatenate([ar, ar, ac, ac], axis=-1)
    sign = jnp.concatenate([-jnp.ones(n_freq), jnp.ones(n_freq)] * 2).astype(F32)
    cos, sin = jnp.cos(ang), jnp.sin(ang) * sign
    reps = LANE // rot_dim
    cos, sin = jnp.tile(cos, (1, reps)), jnp.tile(sin, (1, reps))
    cos = jnp.concatenate([cos, jnp.ones((pad, LANE), F32)], axis=0)
    sin = jnp.concatenate([sin, jnp.zeros((pad, LANE), F32)], axis=0)
    return cos, sin


def kernel(x, c, ctx, c_ctx, w_mod, b_mod, norm1_gain, norm2_gain, w_in, a_v_gain, a_w_s, a_b_s,
           b_q_gain, b_kv_gain, b_w_uq, b_w_ukv, b_qn_gain, b_kn_gain, c_qn_gain, c_kn_gain, c_sink,
           w_out, peer_w_q, peer_subkeys, peer_u, peer_v):
    batch, seq, d = x.shape
    ctx_len = ctx.shape[1]
    depth = w_mod.shape[0]
    a_width = a_v_gain.shape[1]
    q_lora, kv_lora = b_q_gain.shape[1], b_kv_gain.shape[1]
    b_heads = b_w_uq.shape[2] // B_QK
    c_heads = c_sink.shape[1]
    c_kv = c_heads // 3
    p_a = 2 * a_width
    p_b = q_lora + kv_lora + QK_ROPE
    p_b_pad = -(-p_b // (2 * LANE)) * (2 * LANE)
    assert b_heads % 2 == 0 and (q_lora + kv_lora) % LANE == 0
    assert seq % WINDOW == 0 and ctx_len % WINDOW == 0

    rows_all = _Rows(batch, seq, ctx_len)
    bm_tab = rows_all.block(ROW_BLOCK)
    cos_b, sin_b = _rope_tables(seq, QK_ROPE, bm_tab)
    cos_c, sin_c = _rope_tables(seq, HEAD_DIM, bm_tab)

    c8 = jnp.zeros((8, d), F32).at[:batch].set(c).at[batch].set(c_ctx)
    xs = jnp.concatenate([x.reshape(batch * seq, d), ctx.reshape(batch * ctx_len, d)], axis=0)

    for layer in range(depth):
        need_ctx = layer < depth - 1
        mod3 = _modulation(c8, w_mod, b_mod[layer], layer).reshape(8, 1, N_MOD * d)

        w_l = w_in[layer]
        w_a = w_l[:, :p_a].astype(BF16)
        w_b = jnp.pad(w_l[:, p_a:p_a + p_b], ((0, 0), (0, p_b_pad - p_b))).astype(BF16)
        w_c = w_l[:, p_a + p_b:].astype(BF16)
        w_uq = b_w_uq[layer].reshape(q_lora, b_heads, B_QK)
        w_uq = jnp.concatenate([w_uq[:, :, :QK_NOPE].reshape(q_lora, -1),
                                w_uq[:, :, QK_NOPE:].reshape(q_lora, -1)], axis=1).astype(BF16)
        w_ukv = b_w_ukv[layer].reshape(kv_lora, b_heads, QK_NOPE + HEAD_DIM)
        w_ukv = jnp.concatenate([w_ukv[:, :, :QK_NOPE].reshape(kv_lora, -1),
                                 w_ukv[:, :, QK_NOPE:].reshape(kv_lora, -1)], axis=1).astype(BF16)

        (h,) = _norm_mod(xs, norm1_gain[layer], mod3, 0, 1, rows_all, [BF16])
        pa = _matmul(h, w_a)
        pb = _matmul(h, w_b)
        pc = _matmul(h, w_c, bn_target=1280)

        oa = _gmlp(pa, a_v_gain[layer], a_w_s[layer], a_b_s[layer])

        cq, ckv = _mla_pre(pb, b_q_gain[layer], b_kv_gain[layer], rows_all)
        q_raw = _matmul(cq, w_uq)
        kv_raw = _matmul(ckv, w_ukv)
        qh, kh, vh = _mla_post(q_raw, kv_raw, pb, (q_lora + kv_lora) // LANE, b_qn_gain[layer],
                               b_kn_gain[layer], cos_b, sin_b, rows_all, b_heads)
        ob = _mla_attention(qh, kh, vh, rows_all, b_heads, True)

        qc, kc, vc = _gqa_post(pc, c_qn_gain[layer], c_kn_gain[layer], cos_c, sin_c, rows_all,
                               c_heads, c_kv)
        oc = _gqa_attention(qc, kc, vc, c_sink[layer], rows_all, c_heads, c_kv, True)

        if need_ctx:
            rows = rows_all
            ob = jnp.concatenate([ob, _mla_attention(qh, kh, vh, rows_all, b_heads, False)], axis=0)
            oc = jnp.concatenate(
                [oc, _gqa_attention(qc, kc, vc, c_sink[layer], rows_all, c_heads, c_kv, False)], axis=0)
        else:
            rows = _Rows(batch, seq, 0)
        w_o = w_out[layer]
        n_b = b_heads * HEAD_DIM
        w_parts = [w_o[:a_width].astype(BF16), w_o[a_width:a_width + n_b].astype(BF16),
                   w_o[a_width + n_b:].astype(BF16)]
        xs = _matmul_residual([oa, ob, oc], w_parts, xs, mod3, 2, rows)

        g_bf, g = _norm_mod(xs, norm2_gain[layer], mod3, 3, 4, rows, [BF16, F32])
        pq = _matmul(g_bf, peer_w_q[layer].astype(BF16))
        idx, gate = _peer_route(pq, peer_subkeys[layer])
        n_sel = idx.shape[0] * idx.shape[1]
        idx = idx.reshape(n_sel, -1).T
        gate = gate.reshape(n_sel, -1).T
        y = _peer_experts(idx, gate, g, _pack_tables(peer_u, peer_v, layer))
        xs = _gated_add(xs, y, mod3, 5, rows)

    return xs[:batch * seq].reshape(batch, seq, d)
```

```python
import functools

import jax
import jax.numpy as jnp
from jax import lax
from jax.experimental import pallas as pl
from jax.experimental.pallas import tpu as pltpu

F32 = jnp.float32
BF16 = jnp.bfloat16

LANE = 128
SUBLANE = 8
HEAD_DIM = 128
QK_NOPE = 128
QK_ROPE = 64
B_QK = QK_NOPE + QK_ROPE
B_QK_PAD = 2 * LANE
GRID_W = 64
WINDOW = 128
CHUNK = 128
PEER_TOPK = 16
N_MOD = 6
ROPE_BASE = 10000.0
EPS = 1e-6
NEG = -1e30
LOG2E = 1.4426950408889634
MIB = 1024 * 1024
VMEM_LIMIT = 56 * MIB
ROW_BLOCK = 512


def _cparams(semantics, vmem=VMEM_LIMIT):
    return pltpu.CompilerParams(dimension_semantics=semantics, vmem_limit_bytes=vmem)


def _pick(n, target, align=LANE):
    if n <= target:
        return n
    best = None
    d = align
    while d <= target:
        if n % d == 0:
            best = d
        d += align
    assert best is not None, (n, target, align)
    return best


def _gelu(x):
    return 0.5 * x * (1.0 + jnp.tanh(0.7978845608028654 * (x + 0.044715 * (x * x * x))))


def _dot_nt(a, b):
    return lax.dot_general(a, b, (((1,), (1,)), ((), ())), preferred_element_type=F32)


def _modulation_kernel(c_ref, w_ref, b_ref, o_ref, acc_ref):
    k = pl.program_id(1)

    @pl.when(k == 0)
    def _():
        acc_ref[...] = jnp.zeros_like(acc_ref)

    c = c_ref[...]
    silu = c / (1.0 + jnp.exp(-c))
    acc_ref[...] += jnp.dot(silu.astype(BF16), w_ref[...].astype(BF16), preferred_element_type=F32)

    @pl.when(k == pl.num_programs(1) - 1)
    def _():
        o_ref[...] = acc_ref[...] + b_ref[...]


def _modulation(c8, w_all, b, layer):
    rows, d = c8.shape
    n = w_all.shape[2]
    bn = _pick(n, 2048)
    bk = _pick(d, 1024)
    return pl.pallas_call(
        _modulation_kernel,
        grid=(n // bn, d // bk),
        in_specs=[pl.BlockSpec((rows, bk), lambda j, k: (0, k)),
                  pl.BlockSpec((None, bk, bn), lambda j, k: (layer, k, j)),
                  pl.BlockSpec((1, bn), lambda j, k: (0, j))],
        out_specs=pl.BlockSpec((rows, bn), lambda j, k: (0, j)),
        out_shape=jax.ShapeDtypeStruct((rows, n), F32),
        scratch_shapes=[pltpu.VMEM((rows, bn), F32)],
        compiler_params=_cparams(("parallel", "arbitrary")),
        name="modulation",
    )(c8, w_all, b.reshape(1, n))


class _Rows:
    def __init__(self, batch, seq, ctx_len):
        self.batch, self.seq, self.ctx_len = batch, seq, ctx_len
        self.m_lat = batch * seq
        self.m_ctx = batch * ctx_len
        self.m = self.m_lat + self.m_ctx

    def block(self, target):
        bm = target
        while self.seq % bm or self.m_ctx % bm:
            bm //= 2
        assert bm >= 8
        return bm

    def mod_row(self, i, bm):
        return jnp.where(i < self.m_lat // bm, (i * bm) // self.seq, self.batch)


def _norm_mod_kernel(x_ref, gain_ref, shift_ref, scale_ref, *out_refs):
    x = x_ref[...]
    y = x * lax.rsqrt(jnp.mean(x * x, axis=-1, keepdims=True) + EPS) * gain_ref[...]
    h = y * (1.0 + scale_ref[0]) + shift_ref[0]
    for o in out_refs:
        o[...] = h.astype(o.dtype)


def _norm_mod(x, gain, mod3, shift_chunk, scale_chunk, rows, out_dtypes):
    m, d = x.shape
    bm = rows.block(ROW_BLOCK)
    outs = pl.pallas_call(
        _norm_mod_kernel,
        grid=(m // bm,),
        in_specs=[pl.BlockSpec((bm, d), lambda i: (i, 0)),
                  pl.BlockSpec((1, d), lambda i: (0, 0)),
                  pl.BlockSpec((1, 1, d), lambda i: (rows.mod_row(i, bm), 0, shift_chunk)),
                  pl.BlockSpec((1, 1, d), lambda i: (rows.mod_row(i, bm), 0, scale_chunk))],
        out_specs=[pl.BlockSpec((bm, d), lambda i: (i, 0)) for _ in out_dtypes],
        out_shape=[jax.ShapeDtypeStruct((m, d), dt) for dt in out_dtypes],
        compiler_params=_cparams(("parallel",)),
        name="norm_mod",
    )(x, gain.reshape(1, d), mod3, mod3)
    return outs


def _mm_kernel(a_ref, w_ref, o_ref):
    o_ref[...] = jnp.dot(a_ref[...], w_ref[...], preferred_element_type=F32).astype(o_ref.dtype)


def _mm_res_kernel(*refs, parts):
    a_refs, w_refs = refs[:parts], refs[parts:2 * parts]
    res_ref, gate_ref, o_ref = refs[2 * parts:]
    acc = jnp.dot(a_refs[0][...], w_refs[0][...], preferred_element_type=F32)
    for a_ref, w_ref in zip(a_refs[1:], w_refs[1:]):
        acc = acc + jnp.dot(a_ref[...], w_ref[...], preferred_element_type=F32)
    o_ref[...] = res_ref[...] + gate_ref[0] * acc


def _matmul(a, w, out_dtype=F32, bm_target=1024, bn_target=1024):
    m, k = a.shape
    n = w.shape[1]
    bm = _pick(m, bm_target, 8)
    bn = _pick(n, bn_target)
    return pl.pallas_call(
        _mm_kernel,
        grid=(m // bm, n // bn),
        in_specs=[pl.BlockSpec((bm, k), lambda i, j: (i, 0)),
                  pl.BlockSpec((k, bn), lambda i, j: (0, j))],
        out_specs=pl.BlockSpec((bm, bn), lambda i, j: (i, j)),
        out_shape=jax.ShapeDtypeStruct((m, n), out_dtype),
        compiler_params=_cparams(("parallel", "parallel")),
        name="matmul",
    )(a, w)


def _gated_add_kernel(res_ref, y_ref, gate_ref, o_ref):
    o_ref[...] = res_ref[...] + gate_ref[0] * y_ref[...]


def _gated_add(res, y, mod3, gate_chunk, rows):
    m, d = y.shape
    bm = rows.block(ROW_BLOCK)
    return pl.pallas_call(
        _gated_add_kernel,
        grid=(m // bm,),
        in_specs=[pl.BlockSpec((bm, d), lambda i: (i, 0)),
                  pl.BlockSpec((bm, d), lambda i: (i, 0)),
                  pl.BlockSpec((1, 1, d), lambda i: (rows.mod_row(i, bm), 0, gate_chunk))],
        out_specs=pl.BlockSpec((bm, d), lambda i: (i, 0)),
        out_shape=jax.ShapeDtypeStruct((m, d), F32),
        compiler_params=_cparams(("parallel",)),
        name="gated_add",
    )(res, y, mod3)


def _matmul_residual(a_parts, w_parts, res, mod3, gate_chunk, rows):
    m, n = rows.m, res.shape[1]
    bm = rows.block(1024)
    bn = _pick(n, 1024)
    parts = len(a_parts)
    a_specs = [pl.BlockSpec((bm, a.shape[1]), lambda i, j: (i, 0)) for a in a_parts]
    w_specs = [pl.BlockSpec((w.shape[0], bn), lambda i, j: (0, j)) for w in w_parts]
    return pl.pallas_call(
        functools.partial(_mm_res_kernel, parts=parts),
        grid=(m // bm, n // bn),
        in_specs=a_specs + w_specs + [
            pl.BlockSpec((bm, bn), lambda i, j: (i, j)),
            pl.BlockSpec((1, 1, bn), lambda i, j: (rows.mod_row(i, bm), 0, gate_chunk * (n // bn) + j))],
        out_specs=pl.BlockSpec((bm, bn), lambda i, j: (i, j)),
        out_shape=jax.ShapeDtypeStruct((m, n), F32),
        compiler_params=_cparams(("parallel", "parallel")),
        name="matmul_residual",
    )(*a_parts, *w_parts, res, mod3)


def _gmlp_kernel(p_ref, gain_ref, ws_ref, bs_ref, o_ref, *, width, chunks):
    ws = ws_ref[...].astype(BF16)
    bs = bs_ref[...]
    for c in range(chunks):
        r0 = c * CHUNK
        z = _gelu(p_ref[r0:r0 + CHUNK, :])
        u = z[:, :width]
        parts = []
        for h in range(width // HEAD_DIM):
            vh = z[:, width + h * HEAD_DIM: width + (h + 1) * HEAD_DIM]
            vh = vh * lax.rsqrt(jnp.mean(vh * vh, axis=-1, keepdims=True) + EPS)
            parts.append((vh * gain_ref[:, h * HEAD_DIM:(h + 1) * HEAD_DIM]).astype(BF16))
        vn = jnp.concatenate(parts, axis=1)
        s = jnp.dot(ws, vn, preferred_element_type=F32) + bs
        o_ref[r0:r0 + CHUNK, :] = (u * s).astype(o_ref.dtype)


def _gmlp(pa, v_gain, w_s, b_s):
    m, two_w = pa.shape
    width = two_w // 2
    chunks = 4 if m % (4 * CHUNK) == 0 else 2
    bm = chunks * CHUNK
    return pl.pallas_call(
        functools.partial(_gmlp_kernel, width=width, chunks=chunks),
        grid=(m // bm,),
        in_specs=[pl.BlockSpec((bm, two_w), lambda i: (i, 0)),
                  pl.BlockSpec((1, width), lambda i: (0, 0)),
                  pl.BlockSpec((CHUNK, CHUNK), lambda i: (0, 0)),
                  pl.BlockSpec((CHUNK, 1), lambda i: (0, 0))],
        out_specs=pl.BlockSpec((bm, width), lambda i: (i, 0)),
        out_shape=jax.ShapeDtypeStruct((m, width), BF16),
        compiler_params=_cparams(("parallel",)),
        name="gmlp",
    )(pa, v_gain.reshape(1, width), w_s, b_s.reshape(CHUNK, 1))


def _rope(x, cos, sin_signed, half):
    lane = lax.broadcasted_iota(jnp.int32, x.shape, 1)
    first = (lane & (2 * half - 1)) < half
    partner = jnp.where(first, pltpu.roll(x, LANE - half, axis=1), pltpu.roll(x, half, axis=1))
    return x * cos + partner * sin_signed


def _mla_pre_kernel(p_ref, qg_ref, kvg_ref, cq_ref, ckv_ref, *, q_lora, kv_lora):
    cq = p_ref[:, :q_lora]
    cq_ref[...] = (cq * lax.rsqrt(jnp.mean(cq * cq, axis=-1, keepdims=True) + EPS)
                   * qg_ref[...]).astype(cq_ref.dtype)
    ckv = p_ref[:, q_lora:q_lora + kv_lora]
    ckv_ref[...] = (ckv * lax.rsqrt(jnp.mean(ckv * ckv, axis=-1, keepdims=True) + EPS)
                    * kvg_ref[...]).astype(ckv_ref.dtype)


def _mla_pre(pb, q_gain, kv_gain, rows):
    m, n = pb.shape
    q_lora, kv_lora = q_gain.shape[0], kv_gain.shape[0]
    bm = rows.block(ROW_BLOCK)
    return pl.pallas_call(
        functools.partial(_mla_pre_kernel, q_lora=q_lora, kv_lora=kv_lora),
        grid=(m // bm,),
        in_specs=[pl.BlockSpec((bm, n), lambda i: (i, 0)),
                  pl.BlockSpec((1, q_lora), lambda i: (0, 0)),
                  pl.BlockSpec((1, kv_lora), lambda i: (0, 0))],
        out_specs=[pl.BlockSpec((bm, q_lora), lambda i: (i, 0)),
                   pl.BlockSpec((bm, kv_lora), lambda i: (i, 0))],
        out_shape=[jax.ShapeDtypeStruct((m, q_lora), BF16),
                   jax.ShapeDtypeStruct((m, kv_lora), BF16)],
        compiler_params=_cparams(("parallel",)),
        name="mla_pre",
    )(pb, q_gain.reshape(1, q_lora), kv_gain.reshape(1, kv_lora))


def _mla_post_kernel(q_ref, kv_ref, kr_ref, qgn_ref, qgr_ref, kgn_ref, kgr_ref, cos_ref, sin_ref,
                     qh_ref, kh_ref, vh_ref, *, heads):
    cos, sin = cos_ref[...], sin_ref[...]
    lane = lax.broadcasted_iota(jnp.int32, cos.shape, 1)
    low = lane < QK_ROPE
    half = QK_ROPE // 4
    zeros = jnp.zeros(cos.shape, F32)

    kr = jnp.where(low, kr_ref[...], 0.0)
    kr_ss = jnp.sum(kr * kr, axis=-1, keepdims=True)
    kr_rot = _rope(kr * kgr_ref[...], cos, sin, half)

    for hp in range(heads // 2):
        qr = q_ref[:, heads * QK_NOPE + hp * LANE: heads * QK_NOPE + (hp + 1) * LANE]
        qr2 = qr * qr
        ss_lo = jnp.sum(jnp.where(low, qr2, 0.0), axis=-1, keepdims=True)
        ss_hi = jnp.sum(jnp.where(low, 0.0, qr2), axis=-1, keepdims=True)
        rinv = []
        for j, ss_r in enumerate((ss_lo, ss_hi)):
            h = 2 * hp + j
            qn = q_ref[:, h * QK_NOPE:(h + 1) * QK_NOPE]
            r = lax.rsqrt((jnp.sum(qn * qn, axis=-1, keepdims=True) + ss_r) * (1.0 / B_QK) + EPS)
            rinv.append(r)
            qh_ref[h, :, :QK_NOPE] = (qn * r * qgn_ref[...]).astype(qh_ref.dtype)
        qrot = _rope(qr * jnp.where(low, rinv[0], rinv[1]) * qgr_ref[...], cos, sin, half)
        qh_ref[2 * hp, :, QK_NOPE:] = jnp.where(low, qrot, zeros).astype(qh_ref.dtype)
        qh_ref[2 * hp + 1, :, QK_NOPE:] = jnp.where(
            low, pltpu.roll(qrot, QK_ROPE, axis=1), zeros).astype(qh_ref.dtype)

    for h in range(heads):
        kn = kv_ref[:, h * QK_NOPE:(h + 1) * QK_NOPE]
        r = lax.rsqrt((jnp.sum(kn * kn, axis=-1, keepdims=True) + kr_ss) * (1.0 / B_QK) + EPS)
        kh_ref[h, :, :QK_NOPE] = (kn * r * kgn_ref[...]).astype(kh_ref.dtype)
        kh_ref[h, :, QK_NOPE:] = (kr_rot * r).astype(kh_ref.dtype)
        vh_ref[h] = kv_ref[:, (heads + h) * HEAD_DIM:(heads + h + 1) * HEAD_DIM].astype(vh_ref.dtype)


def _mla_post(q_raw, kv_raw, pb, kr_block, qn_gain, kn_gain, cos_t, sin_t, rows, heads):
    m = q_raw.shape[0]
    bm = rows.block(ROW_BLOCK)
    n_lat, n_tab = rows.m_lat // bm, rows.seq // bm
    tab = lambda i: (jnp.where(i < n_lat, i % n_tab, n_tab), 0)
    pair = lambda g: jnp.concatenate([g, g]).reshape(1, LANE)
    return pl.pallas_call(
        functools.partial(_mla_post_kernel, heads=heads),
        grid=(m // bm,),
        in_specs=[pl.BlockSpec((bm, q_raw.shape[1]), lambda i: (i, 0)),
                  pl.BlockSpec((bm, kv_raw.shape[1]), lambda i: (i, 0)),
                  pl.BlockSpec((bm, LANE), lambda i: (i, kr_block)),
                  pl.BlockSpec((1, QK_NOPE), lambda i: (0, 0)),
                  pl.BlockSpec((1, LANE), lambda i: (0, 0)),
                  pl.BlockSpec((1, QK_NOPE), lambda i: (0, 0)),
                  pl.BlockSpec((1, LANE), lambda i: (0, 0)),
                  pl.BlockSpec((bm, LANE), tab),
                  pl.BlockSpec((bm, LANE), tab)],
        out_specs=[pl.BlockSpec((heads, bm, B_QK_PAD), lambda i: (0, i, 0)),
                   pl.BlockSpec((heads, bm, B_QK_PAD), lambda i: (0, i, 0)),
                   pl.BlockSpec((heads, bm, HEAD_DIM), lambda i: (0, i, 0))],
        out_shape=[jax.ShapeDtypeStruct((heads, m, B_QK_PAD), BF16),
                   jax.ShapeDtypeStruct((heads, m, B_QK_PAD), BF16),
                   jax.ShapeDtypeStruct((heads, m, HEAD_DIM), BF16)],
        compiler_params=_cparams(("parallel",)),
        name="mla_post",
    )(q_raw, kv_raw, pb, qn_gain[:QK_NOPE].reshape(1, QK_NOPE), pair(qn_gain[QK_NOPE:]),
      kn_gain[:QK_NOPE].reshape(1, QK_NOPE), pair(kn_gain[QK_NOPE:]), cos_t, sin_t)


def _mla_attn_kernel(q_ref, kc_ref, vc_ref, *rest, with_lat):
    if with_lat:
        kl_ref, vl_ref, o_ref = rest
    else:
        (o_ref,) = rest
    c = (B_QK ** -0.5) * LOG2E
    tn = (((0,), (0,)), ((), ()))
    q = q_ref[...]
    s_c = _dot_nt(kc_ref[...], q)
    m = jnp.max(s_c, axis=0, keepdims=True)
    if with_lat:
        s_l = _dot_nt(kl_ref[...], q)
        m = jnp.maximum(m, jnp.max(s_l, axis=0, keepdims=True))
    p_c = jnp.exp2((s_c - m) * c)
    den = jnp.sum(p_c, axis=0, keepdims=True)
    acc = lax.dot_general(vc_ref[...], p_c.astype(BF16), tn, preferred_element_type=F32)
    if with_lat:
        p_l = jnp.exp2((s_l - m) * c)
        den = den + jnp.sum(p_l, axis=0, keepdims=True)
        acc = acc + lax.dot_general(vl_ref[...], p_l.astype(BF16), tn, preferred_element_type=F32)
    o_ref[...] = (acc / den).T.astype(o_ref.dtype)


def _mla_attention(qh, kh, vh, rows, heads, latent_queries):
    b, s, c = rows.batch, rows.seq, rows.ctx_len
    ctx_blk0 = rows.m_lat // c
    if latent_queries:
        tq = _pick(s, 2048, 8)
        nq = s // tq
        q_map = lambda bi, h, i: (h, bi * nq + i, 0)
        o_map = lambda bi, h, i: (bi * nq + i, h)
        m_out = rows.m_lat
    else:
        tq, nq = c, 1
        q_map = lambda bi, h, i: (h, ctx_blk0 + bi, 0)
        o_map = lambda bi, h, i: (bi, h)
        m_out = rows.m_ctx
    in_specs = [pl.BlockSpec((None, tq, B_QK_PAD), q_map),
                pl.BlockSpec((None, c, B_QK_PAD), lambda bi, h, i: (h, ctx_blk0 + bi, 0)),
                pl.BlockSpec((None, c, HEAD_DIM), lambda bi, h, i: (h, ctx_blk0 + bi, 0))]
    args = [qh, kh, vh]
    if latent_queries:
        in_specs += [pl.BlockSpec((None, s, B_QK_PAD), lambda bi, h, i: (h, bi, 0)),
                     pl.BlockSpec((None, s, HEAD_DIM), lambda bi, h, i: (h, bi, 0))]
        args += [kh, vh]
    return pl.pallas_call(
        functools.partial(_mla_attn_kernel, with_lat=latent_queries),
        grid=(b, heads, nq),
        in_specs=in_specs,
        out_specs=pl.BlockSpec((tq, HEAD_DIM), o_map),
        out_shape=jax.ShapeDtypeStruct((m_out, heads * HEAD_DIM), BF16),
        compiler_params=_cparams(("parallel", "parallel", "parallel")),
        name="mla_attention",
    )(*args)


def _gqa_post_kernel(p_ref, qg_ref, kg_ref, cos_ref, sin_ref, q_ref, k_ref, v_ref, *, q_heads, kv_heads):
    cos, sin = cos_ref[...], sin_ref[...]
    half = HEAD_DIM // 4

    def norm_rope(x, gain):
        xn = x * lax.rsqrt(jnp.mean(x * x, axis=-1, keepdims=True) + EPS) * gain
        return _rope(xn, cos, sin, half)

    for h in range(q_heads):
        sl = slice(h * HEAD_DIM, (h + 1) * HEAD_DIM)
        q_ref[:, sl] = norm_rope(p_ref[:, sl], qg_ref[...]).astype(q_ref.dtype)
    for h in range(kv_heads):
        sl = slice(h * HEAD_DIM, (h + 1) * HEAD_DIM)
        k0 = q_heads * HEAD_DIM
        v0 = (q_heads + kv_heads) * HEAD_DIM
        k_ref[:, sl] = norm_rope(p_ref[:, k0 + h * HEAD_DIM:k0 + (h + 1) * HEAD_DIM],
                                 kg_ref[...]).astype(k_ref.dtype)
        v_ref[:, sl] = p_ref[:, v0 + h * HEAD_DIM:v0 + (h + 1) * HEAD_DIM].astype(v_ref.dtype)


def _gqa_post(pc, qn_gain, kn_gain, cos_t, sin_t, rows, q_heads, kv_heads):
    m, n = pc.shape
    bm = rows.block(ROW_BLOCK)
    n_lat, n_tab = rows.m_lat // bm, rows.seq // bm
    tab = lambda i: (jnp.where(i < n_lat, i % n_tab, n_tab), 0)
    return pl.pallas_call(
        functools.partial(_gqa_post_kernel, q_heads=q_heads, kv_heads=kv_heads),
        grid=(m // bm,),
        in_specs=[pl.BlockSpec((bm, n), lambda i: (i, 0)),
                  pl.BlockSpec((1, HEAD_DIM), lambda i: (0, 0)),
                  pl.BlockSpec((1, HEAD_DIM), lambda i: (0, 0)),
                  pl.BlockSpec((bm, LANE), tab),
                  pl.BlockSpec((bm, LANE), tab)],
        out_specs=[pl.BlockSpec((bm, q_heads * HEAD_DIM), lambda i: (i, 0)),
                   pl.BlockSpec((bm, kv_heads * HEAD_DIM), lambda i: (i, 0)),
                   pl.BlockSpec((bm, kv_heads * HEAD_DIM), lambda i: (i, 0))],
        out_shape=[jax.ShapeDtypeStruct((m, q_heads * HEAD_DIM), BF16),
                   jax.ShapeDtypeStruct((m, kv_heads * HEAD_DIM), BF16),
                   jax.ShapeDtypeStruct((m, kv_heads * HEAD_DIM), BF16)],
        compiler_params=_cparams(("parallel",)),
        name="gqa_post",
    )(pc, qn_gain.reshape(1, HEAD_DIM), kn_gain.reshape(1, HEAD_DIM), cos_t, sin_t)


def _gqa_attn_kernel(sink_ref, q_ref, kc_ref, vc_ref, *rest, group, seq, banded):
    if banded:
        kp_ref, kq_ref, kn_ref, vp_ref, vq_ref, vn_ref, o_ref = rest
    else:
        (o_ref,) = rest
    scale = HEAD_DIM ** -0.5
    blk = pl.program_id(1)
    rows_q = group * WINDOW
    kv_heads = kc_ref.shape[1] // HEAD_DIM
    row = lax.broadcasted_iota(jnp.int32, (rows_q, 1), 0)

    for kvh in range(kv_heads):
        hs = slice(kvh * HEAD_DIM, (kvh + 1) * HEAD_DIM)
        q0 = kvh * group * HEAD_DIM
        q = jnp.concatenate([q_ref[:, q0 + g * HEAD_DIM:q0 + (g + 1) * HEAD_DIM]
                             for g in range(group)], axis=0)
        sink = jnp.zeros((rows_q, 1), F32)
        for g in range(group):
            in_g = (row >= g * WINDOW) & (row < (g + 1) * WINDOW)
            sink = jnp.where(in_g, sink_ref[kvh * group + g], sink)

        s_c = _dot_nt(q, kc_ref[:, hs]) * scale
        m = jnp.maximum(jnp.max(s_c, axis=-1, keepdims=True), sink)
        if banded:
            kb = jnp.concatenate([kp_ref[:, hs], kq_ref[:, hs], kn_ref[:, hs]], axis=0)
            vb = jnp.concatenate([vp_ref[:, hs], vq_ref[:, hs], vn_ref[:, hs]], axis=0)
            s_b = _dot_nt(q, kb) * scale
            qpos = lax.broadcasted_iota(jnp.int32, s_b.shape, 0) & (WINDOW - 1)
            krel = lax.broadcasted_iota(jnp.int32, s_b.shape, 1) - WINDOW
            kpos = blk * WINDOW + krel
            valid = (jnp.abs(qpos - krel) <= WINDOW) & (kpos >= 0) & (kpos < seq)
            s_b = jnp.where(valid, s_b, NEG)
            m = jnp.maximum(m, jnp.max(s_b, axis=-1, keepdims=True))
        p_c = jnp.exp(s_c - m)
        den = jnp.sum(p_c, axis=-1, keepdims=True) + jnp.exp(sink - m)
        acc = jnp.dot(p_c.astype(BF16), vc_ref[:, hs], preferred_element_type=F32)
        if banded:
            p_b = jnp.exp(s_b - m)
            den = den + jnp.sum(p_b, axis=-1, keepdims=True)
            acc = acc + jnp.dot(p_b.astype(BF16), vb, preferred_element_type=F32)
        o = acc / den
        for g in range(group):
            o_ref[:, q0 + g * HEAD_DIM:q0 + (g + 1) * HEAD_DIM] = (
                o[g * WINDOW:(g + 1) * WINDOW].astype(o_ref.dtype))


def _gqa_attention(qc, kc, vc, sink, rows, q_heads, kv_heads, latent_queries):
    b, s, c = rows.batch, rows.seq, rows.ctx_len
    group = q_heads // kv_heads
    qw, kw = q_heads * HEAD_DIM, kv_heads * HEAD_DIM
    ctx_blk0 = rows.m_lat // c
    ctx_spec = lambda: pl.BlockSpec((c, kw), lambda bi, n: (ctx_blk0 + bi, 0))
    if latent_queries:
        nb = s // WINDOW
        qrow = lambda bi, n: bi * nb + n
        m_out = rows.m_lat
        orow = qrow
    else:
        nb = c // WINDOW
        qrow = lambda bi, n: rows.m_lat // WINDOW + bi * nb + n
        orow = lambda bi, n: bi * nb + n
        m_out = rows.m_ctx
    in_specs = [pl.BlockSpec(memory_space=pltpu.SMEM),
                pl.BlockSpec((WINDOW, qw), lambda bi, n: (qrow(bi, n), 0)),
                ctx_spec(), ctx_spec()]
    args = [sink, qc, kc, vc]
    if latent_queries:
        band = [lambda bi, n: (bi * nb + jnp.maximum(n - 1, 0), 0),
                lambda bi, n: (bi * nb + n, 0),
                lambda bi, n: (bi * nb + jnp.minimum(n + 1, nb - 1), 0)]
        in_specs += [pl.BlockSpec((WINDOW, kw), f) for f in band] * 2
        args += [kc, kc, kc, vc, vc, vc]
    return pl.pallas_call(
        functools.partial(_gqa_attn_kernel, group=group, seq=s, banded=latent_queries),
        grid=(b, nb),
        in_specs=in_specs,
        out_specs=pl.BlockSpec((WINDOW, qw), lambda bi, n: (orow(bi, n), 0)),
        out_shape=jax.ShapeDtypeStruct((m_out, qw), BF16),
        compiler_params=_cparams(("parallel", "parallel")),
        name="gqa_attention",
    )(*args)


def _extract_topk(s, pos, ids, count, val_ref, id_ref):
    beyond = 3.0e38
    for r in range(count):
        top = jnp.max(s, axis=0, keepdims=True)
        first = jnp.min(jnp.where(s == top, pos, beyond), axis=0, keepdims=True)
        hit = pos == first
        val_ref[r:r + 1, :] = top
        if ids is None:
            id_ref[r:r + 1, :] = first
        else:
            id_ref[r:r + 1, :] = jnp.max(jnp.where(hit, ids, -1.0), axis=0, keepdims=True)
        s = jnp.where(hit, -jnp.inf, s)


_PAIR_SPANS = [(a, PEER_TOPK // (a + 1)) for a in range(PEER_TOPK)]
_N_PAIRS = sum(n for _, n in _PAIR_SPANS)
_N_PAIRS_PAD = -(-_N_PAIRS // 8) * 8


def _route_kernel(q_ref, sk_ref, idx_ref, gate_ref, tv_ref, ti_ref, cs_ref, cp_ref, ci_ref, bs_ref,
                  bi_ref, *, n_keys):
    tt = q_ref.shape[0]
    key_id = lax.broadcasted_iota(jnp.int32, (n_keys, tt), 0).astype(F32)
    for p in range(2):
        qp = q_ref[:, p * LANE:(p + 1) * LANE].astype(BF16)
        scores = _dot_nt(sk_ref[p], qp)
        _extract_topk(scores, key_id, None, PEER_TOPK, tv_ref.at[p], ti_ref.at[p])
    cs_ref[...] = jnp.full(cs_ref.shape, -jnp.inf, F32)
    cp_ref[...] = jnp.full(cp_ref.shape, 1.0e6, F32)
    ci_ref[...] = jnp.full(ci_ref.shape, -1.0, F32)
    off = 0
    for a, n in _PAIR_SPANS:
        rs = slice(off, off + n)
        cs_ref[rs, :] = tv_ref[0, a:a + 1, :] + tv_ref[1, 0:n, :]
        cp_ref[rs, :] = float(a * PEER_TOPK) + lax.broadcasted_iota(jnp.int32, (n, tt), 0).astype(F32)
        ci_ref[rs, :] = ti_ref[0, a:a + 1, :] * float(n_keys) + ti_ref[1, 0:n, :]
        off += n
    _extract_topk(cs_ref[...], cp_ref[...], ci_ref[...], PEER_TOPK, bs_ref, bi_ref)
    best = bs_ref[...]
    e = jnp.exp(best - best[0:1, :])
    gate_ref[...] = e / jnp.sum(e, axis=0, keepdims=True)
    idx_ref[...] = bi_ref[...].astype(jnp.int32)


def _peer_route(pq, subkeys):
    m = pq.shape[0]
    heads, _, n_keys, half = subkeys.shape
    assert half == LANE
    tt = _pick(m, 1024)
    k = PEER_TOPK
    return pl.pallas_call(
        functools.partial(_route_kernel, n_keys=n_keys),
        grid=(m // tt, heads),
        in_specs=[pl.BlockSpec((tt, 2 * LANE), lambda i, h: (i, h)),
                  pl.BlockSpec((None, 2, n_keys, half), lambda i, h: (h, 0, 0, 0))],
        out_specs=[pl.BlockSpec((None, k, tt), lambda i, h: (h, 0, i)),
                   pl.BlockSpec((None, k, tt), lambda i, h: (h, 0, i))],
        out_shape=[jax.ShapeDtypeStruct((heads, k, m), jnp.int32),
                   jax.ShapeDtypeStruct((heads, k, m), F32)],
        scratch_shapes=[pltpu.VMEM((2, k, tt), F32), pltpu.VMEM((2, k, tt), F32)]
        + [pltpu.VMEM((_N_PAIRS_PAD, tt), F32)] * 3
        + [pltpu.VMEM((k, tt), F32), pltpu.VMEM((k, tt), F32)],
        compiler_params=_cparams(("parallel", "parallel")),
        name="peer_route",
    )(pq, subkeys.astype(BF16))


HI16 = 0xFFFF0000


def _pack_kernel(u_ref, v_ref, o_ref):
    ub = lax.bitcast_convert_type(u_ref[...].astype(BF16).astype(F32), jnp.uint32)
    vb = lax.bitcast_convert_type(v_ref[...].astype(BF16).astype(F32), jnp.uint32)
    o_ref[...] = (vb & jnp.uint32(HI16)) | (ub >> 16)


def _pack_tables(u_all, v_all, layer):
    _, e, d = u_all.shape
    be = _pick(e, ROW_BLOCK, 8)
    spec = pl.BlockSpec((None, be, d), lambda i: (layer, i, 0))
    return pl.pallas_call(
        _pack_kernel,
        grid=(e // be,),
        in_specs=[spec, spec],
        out_specs=pl.BlockSpec((be, d), lambda i: (i, 0)),
        out_shape=jax.ShapeDtypeStruct((e, d), jnp.uint32),
        compiler_params=_cparams(("parallel",)),
        name="peer_pack",
    )(u_all, v_all)


PEER_RING = 4
PEER_AHEAD = PEER_RING - 1


def _peer_kernel(idx_ref, g_ref, gate_ref, tab_hbm, o_ref, *scratch, tokens, n_sel):
    bufs, sem = scratch[:PEER_RING], scratch[PEER_RING]
    step = pl.program_id(0)
    tile = SUBLANE

    def row_copy(e, s, k):
        return pltpu.make_async_copy(tab_hbm.at[pl.ds(e, 1), :], bufs[s].at[pl.ds(k, 1), :], sem.at[s])

    def wait(s):
        pltpu.make_async_copy(tab_hbm.at[pl.ds(0, n_sel), :], bufs[s], sem.at[s]).wait()

    @pl.when(step == 0)
    def _():
        for t in range(PEER_AHEAD):
            for k in range(n_sel):
                row_copy(idx_ref[t, k], t, k).start()

    eye = (lax.broadcasted_iota(jnp.int32, (n_sel, n_sel), 0)
           == lax.broadcasted_iota(jnp.int32, (n_sel, n_sel), 1))

    def process(t, s):
        cur, s_nxt = bufs[s], (s + PEER_AHEAD) % PEER_RING
        wait(s)
        g = g_ref[pl.ds(t, 1), :]
        parts = []
        for c in range(n_sel // tile):
            for k in range(c * tile, (c + 1) * tile):
                row_copy(idx_ref[t + PEER_AHEAD, k], s_nxt, k).start(priority=k % 2)
            u = lax.bitcast_convert_type(cur[c * tile:(c + 1) * tile, :] << 16, F32)
            parts.append(jnp.sum(u * g, axis=1, keepdims=True))
        d = jnp.concatenate(parts, axis=0)
        gate_row = gate_ref[pl.ds(t, 1), :]
        gate_col = jnp.sum(jnp.where(eye, gate_row, 0.0), axis=1, keepdims=True)
        w = gate_col * _gelu(d)
        acc = jnp.zeros((tile, g.shape[1]), F32)
        for c in range(n_sel // tile):
            v = lax.bitcast_convert_type(cur[c * tile:(c + 1) * tile, :] & jnp.uint32(HI16), F32)
            acc = acc + v * w[c * tile:(c + 1) * tile, :]
        o_ref[pl.ds(t, 1), :] = jnp.sum(acc, axis=0, keepdims=True)

    def ring(i, carry):
        for s in range(PEER_RING):
            process(PEER_RING * i + s, s)
        return carry

    lax.fori_loop(0, tokens // PEER_RING, ring, 0)

    @pl.when(step == pl.num_programs(0) - 1)
    def _():
        for s in range(PEER_AHEAD):
            wait(s)


def _peer_experts(idx, gate, g, tab):
    m, d = g.shape
    n_sel = idx.shape[1]
    tb = _pick(m, 128, 8)
    nb = m // tb
    assert tb % PEER_RING == 0 and PEER_AHEAD <= SUBLANE
    blocks = idx.reshape(nb, tb, n_sel)
    idx_ext = jnp.concatenate([blocks, jnp.roll(blocks[:, :SUBLANE], -1, axis=0)], axis=1)
    return pl.pallas_call(
        functools.partial(_peer_kernel, tokens=tb, n_sel=n_sel),
        grid=(nb,),
        in_specs=[pl.BlockSpec((None, tb + SUBLANE, n_sel), lambda i: (i, 0, 0),
                               memory_space=pltpu.SMEM),
                  pl.BlockSpec((tb, d), lambda i: (i, 0)),
                  pl.BlockSpec((tb, n_sel), lambda i: (i, 0)),
                  pl.BlockSpec(memory_space=pl.ANY)],
        out_specs=pl.BlockSpec((tb, d), lambda i: (i, 0)),
        out_shape=jax.ShapeDtypeStruct((m, d), F32),
        scratch_shapes=[pltpu.VMEM((n_sel, d), jnp.uint32)] * PEER_RING
        + [pltpu.SemaphoreType.DMA((PEER_RING,))],
        compiler_params=_cparams(("arbitrary",)),
        name="peer_experts",
    )(idx_ext, g, gate, tab)


def _rope_tables(seq, rot_dim, pad):
    rows = seq // GRID_W
    row = jnp.repeat(jnp.arange(rows, dtype=F32), GRID_W)
    col = jnp.tile(jnp.arange(GRID_W, dtype=F32), rows)
    n_freq = rot_dim // 4
    inv = ROPE_BASE ** (-jnp.arange(n_freq, dtype=F32) / n_freq)
    ar, ac = row[:, None] * inv, col[:, None] * inv
    ang = jnp.concatenate([ar, ar, ac, ac], axis=-1)
    sign = jnp.concatenate([-jnp.ones(n_freq), jnp.ones(n_freq)] * 2).astype(F32)
    cos, sin = jnp.cos(ang), jnp.sin(ang) * sign
    reps = LANE // rot_dim
    cos, sin = jnp.tile(cos, (1, reps)), jnp.tile(sin, (1, reps))
    cos = jnp.concatenate([cos, jnp.ones((pad, LANE), F32)], axis=0)
    sin = jnp.concatenate([sin, jnp.zeros((pad, LANE), F32)], axis=0)
    return cos, sin


def kernel(x, c, ctx, c_ctx, w_mod, b_mod, norm1_gain, norm2_gain, w_in, a_v_gain, a_w_s, a_b_s,
           b_q_gain, b_kv_gain, b_w_uq, b_w_ukv, b_qn_gain, b_kn_gain, c_qn_gain, c_kn_gain, c_sink,
           w_out, peer_w_q, peer_subkeys, peer_u, peer_v):
    batch, seq, d = x.shape
    ctx_len = ctx.shape[1]
    depth = w_mod.shape[0]
    a_width = a_v_gain.shape[1]
    q_lora, kv_lora = b_q_gain.shape[1], b_kv_gain.shape[1]
    b_heads = b_w_uq.shape[2] // B_QK
    c_heads = c_sink.shape[1]
    c_kv = c_heads // 3
    p_a = 2 * a_width
    p_b = q_lora + kv_lora + QK_ROPE
    p_b_pad = -(-p_b // (2 * LANE)) * (2 * LANE)
    assert b_heads % 2 == 0 and (q_lora + kv_lora) % LANE == 0
    assert seq % WINDOW == 0 and ctx_len % WINDOW == 0

    rows_all = _Rows(batch, seq, ctx_len)
    bm_tab = rows_all.block(ROW_BLOCK)
    cos_b, sin_b = _rope_tables(seq, QK_ROPE, bm_tab)
    cos_c, sin_c = _rope_tables(seq, HEAD_DIM, bm_tab)

    c8 = jnp.zeros((8, d), F32).at[:batch].set(c).at[batch].set(c_ctx)
    xs = jnp.concatenate([x.reshape(batch * seq, d), ctx.reshape(batch * ctx_len, d)], axis=0)

    for layer in range(depth):
        need_ctx = layer < depth - 1
        mod3 = _modulation(c8, w_mod, b_mod[layer], layer).reshape(8, 1, N_MOD * d)

        w_l = w_in[layer]
        w_a = w_l[:, :p_a].astype(BF16)
        w_b = jnp.pad(w_l[:, p_a:p_a + p_b], ((0, 0), (0, p_b_pad - p_b))).astype(BF16)
        w_c = w_l[:, p_a + p_b:].astype(BF16)
        w_uq = b_w_uq[layer].reshape(q_lora, b_heads, B_QK)
        w_uq = jnp.concatenate([w_uq[:, :, :QK_NOPE].reshape(q_lora, -1),
                                w_uq[:, :, QK_NOPE:].reshape(q_lora, -1)], axis=1).astype(BF16)
        w_ukv = b_w_ukv[layer].reshape(kv_lora, b_heads, QK_NOPE + HEAD_DIM)
        w_ukv = jnp.concatenate([w_ukv[:, :, :QK_NOPE].reshape(kv_lora, -1),
                                 w_ukv[:, :, QK_NOPE:].reshape(kv_lora, -1)], axis=1).astype(BF16)

        (h,) = _norm_mod(xs, norm1_gain[layer], mod3, 0, 1, rows_all, [BF16])
        pa = _matmul(h, w_a)
        pb = _matmul(h, w_b, bn_target=1152)
        pc = _matmul(h, w_c, bn_target=1280)

        oa = _gmlp(pa, a_v_gain[layer], a_w_s[layer], a_b_s[layer])

        cq, ckv = _mla_pre(pb, b_q_gain[layer], b_kv_gain[layer], rows_all)
        q_raw = _matmul(cq, w_uq, bm_target=2304, bn_target=1152)
        kv_raw = _matmul(ckv, w_ukv, bm_target=3072)
        qh, kh, vh = _mla_post(q_raw, kv_raw, pb, (q_lora + kv_lora) // LANE, b_qn_gain[layer],
                               b_kn_gain[layer], cos_b, sin_b, rows_all, b_heads)
        ob = _mla_attention(qh, kh, vh, rows_all, b_heads, True)

        qc, kc, vc = _gqa_post(pc, c_qn_gain[layer], c_kn_gain[layer], cos_c, sin_c, rows_all,
                               c_heads, c_kv)
        oc = _gqa_attention(qc, kc, vc, c_sink[layer], rows_all, c_heads, c_kv, True)

        if need_ctx:
            rows = rows_all
            ob = jnp.concatenate([ob, _mla_attention(qh, kh, vh, rows_all, b_heads, False)], axis=0)
            oc = jnp.concatenate(
                [oc, _gqa_attention(qc, kc, vc, c_sink[layer], rows_all, c_heads, c_kv, False)], axis=0)
        else:
            rows = _Rows(batch, seq, 0)
        w_o = w_out[layer].astype(BF16)
        n_b = b_heads * HEAD_DIM
        w_parts = [w_o[:a_width], w_o[a_width:a_width + n_b], w_o[a_width + n_b:]]
        xs = _matmul_residual([oa, ob, oc], w_parts, xs, mod3, 2, rows)

        g_bf, g = _norm_mod(xs, norm2_gain[layer], mod3, 3, 4, rows, [BF16, F32])
        pq = _matmul(g_bf, peer_w_q[layer].astype(BF16))
        idx, gate = _peer_route(pq, peer_subkeys[layer])
        n_sel = idx.shape[0] * idx.shape[1]
        idx = idx.reshape(n_sel, -1).T
        gate = gate.reshape(n_sel, -1).T
        y = _peer_experts(idx, gate, g, _pack_tables(peer_u, peer_v, layer))
        xs = _gated_add(xs, y, mod3, 5, rows)

    return xs[:batch * seq].reshape(batch, seq, d)
```

```python
import functools

import jax
import jax.numpy as jnp
from jax import lax
from jax.experimental import pallas as pl
from jax.experimental.pallas import tpu as pltpu

F32 = jnp.float32
BF16 = jnp.bfloat16

LANE = 128
SUBLANE = 8
HEAD_DIM = 128
QK_NOPE = 128
QK_ROPE = 64
B_QK = QK_NOPE + QK_ROPE
B_QK_PAD = 2 * LANE
GRID_W = 64
WINDOW = 128
CHUNK = 128
PEER_TOPK = 16
N_MOD = 6
ROPE_BASE = 10000.0
EPS = 1e-6
NEG = -1e30
LOG2E = 1.4426950408889634
MIB = 1024 * 1024
VMEM_LIMIT = 56 * MIB
ROW_BLOCK = 512


def _cparams(semantics, vmem=VMEM_LIMIT):
    return pltpu.CompilerParams(dimension_semantics=semantics, vmem_limit_bytes=vmem)


def _pick(n, target, align=LANE):
    if n <= target:
        return n
    best = None
    d = align
    while d <= target:
        if n % d == 0:
            best = d
        d += align
    assert best is not None, (n, target, align)
    return best


def _gelu(x):
    return 0.5 * x * (1.0 + jnp.tanh(0.7978845608028654 * (x + 0.044715 * (x * x * x))))


def _dot_nt(a, b):
    return lax.dot_general(a, b, (((1,), (1,)), ((), ())), preferred_element_type=F32)


def _modulation_kernel(c_ref, w_ref, b_ref, o_ref, acc_ref):
    k = pl.program_id(1)

    @pl.when(k == 0)
    def _():
        acc_ref[...] = jnp.zeros_like(acc_ref)

    c = c_ref[...]
    silu = c / (1.0 + jnp.exp(-c))
    acc_ref[...] += jnp.dot(silu.astype(BF16), w_ref[...].astype(BF16), preferred_element_type=F32)

    @pl.when(k == pl.num_programs(1) - 1)
    def _():
        o_ref[...] = acc_ref[...] + b_ref[...]


def _modulation(c8, w_all, b, layer):
    rows, d = c8.shape
    n = w_all.shape[2]
    bn = _pick(n, 2048)
    bk = _pick(d, 1024)
    return pl.pallas_call(
        _modulation_kernel,
        grid=(n // bn, d // bk),
        in_specs=[pl.BlockSpec((rows, bk), lambda j, k: (0, k)),
                  pl.BlockSpec((None, bk, bn), lambda j, k: (layer, k, j)),
                  pl.BlockSpec((1, bn), lambda j, k: (0, j))],
        out_specs=pl.BlockSpec((rows, bn), lambda j, k: (0, j)),
        out_shape=jax.ShapeDtypeStruct((rows, n), F32),
        scratch_shapes=[pltpu.VMEM((rows, bn), F32)],
        compiler_params=_cparams(("parallel", "arbitrary")),
        name="modulation",
    )(c8, w_all, b.reshape(1, n))


class _Rows:
    def __init__(self, batch, seq, ctx_len):
        self.batch, self.seq, self.ctx_len = batch, seq, ctx_len
        self.m_lat = batch * seq
        self.m_ctx = batch * ctx_len
        self.m = self.m_lat + self.m_ctx

    def block(self, target):
        bm = target
        while self.seq % bm or self.m_ctx % bm:
            bm //= 2
        assert bm >= 8
        return bm

    def mod_row(self, i, bm):
        return jnp.where(i < self.m_lat // bm, (i * bm) // self.seq, self.batch)


def _norm_mod_kernel(x_ref, gain_ref, shift_ref, scale_ref, *out_refs):
    x = x_ref[...]
    y = x * lax.rsqrt(jnp.mean(x * x, axis=-1, keepdims=True) + EPS) * gain_ref[...]
    h = y * (1.0 + scale_ref[0]) + shift_ref[0]
    for o in out_refs:
        o[...] = h.astype(o.dtype)


def _norm_mod(x, gain, mod3, shift_chunk, scale_chunk, rows, out_dtypes):
    m, d = x.shape
    bm = rows.block(ROW_BLOCK)
    outs = pl.pallas_call(
        _norm_mod_kernel,
        grid=(m // bm,),
        in_specs=[pl.BlockSpec((bm, d), lambda i: (i, 0)),
                  pl.BlockSpec((1, d), lambda i: (0, 0)),
                  pl.BlockSpec((1, 1, d), lambda i: (rows.mod_row(i, bm), 0, shift_chunk)),
                  pl.BlockSpec((1, 1, d), lambda i: (rows.mod_row(i, bm), 0, scale_chunk))],
        out_specs=[pl.BlockSpec((bm, d), lambda i: (i, 0)) for _ in out_dtypes],
        out_shape=[jax.ShapeDtypeStruct((m, d), dt) for dt in out_dtypes],
        compiler_params=_cparams(("parallel",)),
        name="norm_mod",
    )(x, gain.reshape(1, d), mod3, mod3)
    return outs


def _mm_kernel(a_ref, w_ref, o_ref):
    o_ref[...] = jnp.dot(a_ref[...], w_ref[...], preferred_element_type=F32).astype(o_ref.dtype)


def _mm_res_kernel(*refs, parts):
    a_refs, w_refs = refs[:parts], refs[parts:2 * parts]
    res_ref, gate_ref, o_ref = refs[2 * parts:]
    acc = jnp.dot(a_refs[0][...], w_refs[0][...], preferred_element_type=F32)
    for a_ref, w_ref in zip(a_refs[1:], w_refs[1:]):
        acc = acc + jnp.dot(a_ref[...], w_ref[...], preferred_element_type=F32)
    o_ref[...] = res_ref[...] + gate_ref[0] * acc


def _matmul(a, w, out_dtype=F32, bm_target=1024, bn_target=1024):
    m, k = a.shape
    n = w.shape[1]
    bm = _pick(m, bm_target, 8)
    bn = _pick(n, bn_target)
    return pl.pallas_call(
        _mm_kernel,
        grid=(m // bm, n // bn),
        in_specs=[pl.BlockSpec((bm, k), lambda i, j: (i, 0)),
                  pl.BlockSpec((k, bn), lambda i, j: (0, j))],
        out_specs=pl.BlockSpec((bm, bn), lambda i, j: (i, j)),
        out_shape=jax.ShapeDtypeStruct((m, n), out_dtype),
        compiler_params=_cparams(("parallel", "parallel")),
        name="matmul",
    )(a, w)


def _gated_add_kernel(res_ref, y_ref, gate_ref, o_ref):
    o_ref[...] = res_ref[...] + gate_ref[0] * y_ref[...]


def _gated_add(res, y, mod3, gate_chunk, rows):
    m, d = y.shape
    bm = rows.block(ROW_BLOCK)
    return pl.pallas_call(
        _gated_add_kernel,
        grid=(m // bm,),
        in_specs=[pl.BlockSpec((bm, d), lambda i: (i, 0)),
                  pl.BlockSpec((bm, d), lambda i: (i, 0)),
                  pl.BlockSpec((1, 1, d), lambda i: (rows.mod_row(i, bm), 0, gate_chunk))],
        out_specs=pl.BlockSpec((bm, d), lambda i: (i, 0)),
        out_shape=jax.ShapeDtypeStruct((m, d), F32),
        compiler_params=_cparams(("parallel",)),
        name="gated_add",
    )(res, y, mod3)


def _matmul_residual(a_parts, w_parts, res, mod3, gate_chunk, rows):
    m, n = rows.m, res.shape[1]
    bm = rows.block(1024)
    bn = _pick(n, 1024)
    parts = len(a_parts)
    a_specs = [pl.BlockSpec((bm, a.shape[1]), lambda i, j: (i, 0)) for a in a_parts]
    w_specs = [pl.BlockSpec((w.shape[0], bn), lambda i, j: (0, j)) for w in w_parts]
    return pl.pallas_call(
        functools.partial(_mm_res_kernel, parts=parts),
        grid=(m // bm, n // bn),
        in_specs=a_specs + w_specs + [
            pl.BlockSpec((bm, bn), lambda i, j: (i, j)),
            pl.BlockSpec((1, 1, bn), lambda i, j: (rows.mod_row(i, bm), 0, gate_chunk * (n // bn) + j))],
        out_specs=pl.BlockSpec((bm, bn), lambda i, j: (i, j)),
        out_shape=jax.ShapeDtypeStruct((m, n), F32),
        compiler_params=_cparams(("parallel", "parallel")),
        name="matmul_residual",
    )(*a_parts, *w_parts, res, mod3)


def _gmlp_kernel(p_ref, gain_ref, ws_ref, bs_ref, o_ref, *, width, chunks):
    ws = ws_ref[...].astype(BF16)
    bs = bs_ref[...]
    for c in range(chunks):
        r0 = c * CHUNK
        z = _gelu(p_ref[r0:r0 + CHUNK, :])
        u = z[:, :width]
        parts = []
        for h in range(width // HEAD_DIM):
            vh = z[:, width + h * HEAD_DIM: width + (h + 1) * HEAD_DIM]
            vh = vh * lax.rsqrt(jnp.mean(vh * vh, axis=-1, keepdims=True) + EPS)
            parts.append((vh * gain_ref[:, h * HEAD_DIM:(h + 1) * HEAD_DIM]).astype(BF16))
        vn = jnp.concatenate(parts, axis=1)
        s = jnp.dot(ws, vn, preferred_element_type=F32) + bs
        o_ref[r0:r0 + CHUNK, :] = (u * s).astype(o_ref.dtype)


def _gmlp(pa, v_gain, w_s, b_s):
    m, two_w = pa.shape
    width = two_w // 2
    chunks = 4 if m % (4 * CHUNK) == 0 else 2
    bm = chunks * CHUNK
    return pl.pallas_call(
        functools.partial(_gmlp_kernel, width=width, chunks=chunks),
        grid=(m // bm,),
        in_specs=[pl.BlockSpec((bm, two_w), lambda i: (i, 0)),
                  pl.BlockSpec((1, width), lambda i: (0, 0)),
                  pl.BlockSpec((CHUNK, CHUNK), lambda i: (0, 0)),
                  pl.BlockSpec((CHUNK, 1), lambda i: (0, 0))],
        out_specs=pl.BlockSpec((bm, width), lambda i: (i, 0)),
        out_shape=jax.ShapeDtypeStruct((m, width), BF16),
        compiler_params=_cparams(("parallel",)),
        name="gmlp",
    )(pa, v_gain.reshape(1, width), w_s, b_s.reshape(CHUNK, 1))


def _rope(x, cos, sin_signed, half):
    lane = lax.broadcasted_iota(jnp.int32, x.shape, 1)
    first = (lane & (2 * half - 1)) < half
    partner = jnp.where(first, pltpu.roll(x, LANE - half, axis=1), pltpu.roll(x, half, axis=1))
    return x * cos + partner * sin_signed


def _mla_pre_kernel(p_ref, qg_ref, kvg_ref, cq_ref, ckv_ref, *, q_lora, kv_lora):
    cq = p_ref[:, :q_lora]
    cq_ref[...] = (cq * lax.rsqrt(jnp.mean(cq * cq, axis=-1, keepdims=True) + EPS)
                   * qg_ref[...]).astype(cq_ref.dtype)
    ckv = p_ref[:, q_lora:q_lora + kv_lora]
    ckv_ref[...] = (ckv * lax.rsqrt(jnp.mean(ckv * ckv, axis=-1, keepdims=True) + EPS)
                    * kvg_ref[...]).astype(ckv_ref.dtype)


def _mla_pre(pb, q_gain, kv_gain, rows):
    m, n = pb.shape
    q_lora, kv_lora = q_gain.shape[0], kv_gain.shape[0]
    bm = rows.block(ROW_BLOCK)
    return pl.pallas_call(
        functools.partial(_mla_pre_kernel, q_lora=q_lora, kv_lora=kv_lora),
        grid=(m // bm,),
        in_specs=[pl.BlockSpec((bm, n), lambda i: (i, 0)),
                  pl.BlockSpec((1, q_lora), lambda i: (0, 0)),
                  pl.BlockSpec((1, kv_lora), lambda i: (0, 0))],
        out_specs=[pl.BlockSpec((bm, q_lora), lambda i: (i, 0)),
                   pl.BlockSpec((bm, kv_lora), lambda i: (i, 0))],
        out_shape=[jax.ShapeDtypeStruct((m, q_lora), BF16),
                   jax.ShapeDtypeStruct((m, kv_lora), BF16)],
        compiler_params=_cparams(("parallel",)),
        name="mla_pre",
    )(pb, q_gain.reshape(1, q_lora), kv_gain.reshape(1, kv_lora))


def _mla_post_kernel(q_ref, kv_ref, kr_ref, qgn_ref, qgr_ref, kgn_ref, kgr_ref, cos_ref, sin_ref,
                     qh_ref, kh_ref, vh_ref, *, heads):
    cos, sin = cos_ref[...], sin_ref[...]
    lane = lax.broadcasted_iota(jnp.int32, cos.shape, 1)
    low = lane < QK_ROPE
    half = QK_ROPE // 4
    zeros = jnp.zeros(cos.shape, F32)

    kr = jnp.where(low, kr_ref[...], 0.0)
    kr_ss = jnp.sum(kr * kr, axis=-1, keepdims=True)
    kr_rot = _rope(kr * kgr_ref[...], cos, sin, half)

    for hp in range(heads // 2):
        qr = q_ref[:, heads * QK_NOPE + hp * LANE: heads * QK_NOPE + (hp + 1) * LANE]
        qr2 = qr * qr
        ss_lo = jnp.sum(jnp.where(low, qr2, 0.0), axis=-1, keepdims=True)
        ss_hi = jnp.sum(jnp.where(low, 0.0, qr2), axis=-1, keepdims=True)
        rinv = []
        for j, ss_r in enumerate((ss_lo, ss_hi)):
            h = 2 * hp + j
            qn = q_ref[:, h * QK_NOPE:(h + 1) * QK_NOPE]
            r = lax.rsqrt((jnp.sum(qn * qn, axis=-1, keepdims=True) + ss_r) * (1.0 / B_QK) + EPS)
            rinv.append(r)
            qh_ref[h, :, :QK_NOPE] = (qn * r * qgn_ref[...]).astype(qh_ref.dtype)
        qrot = _rope(qr * jnp.where(low, rinv[0], rinv[1]) * qgr_ref[...], cos, sin, half)
        qh_ref[2 * hp, :, QK_NOPE:] = jnp.where(low, qrot, zeros).astype(qh_ref.dtype)
        qh_ref[2 * hp + 1, :, QK_NOPE:] = jnp.where(
            low, pltpu.roll(qrot, QK_ROPE, axis=1), zeros).astype(qh_ref.dtype)

    for h in range(heads):
        kn = kv_ref[:, h * QK_NOPE:(h + 1) * QK_NOPE]
        r = lax.rsqrt((jnp.sum(kn * kn, axis=-1, keepdims=True) + kr_ss) * (1.0 / B_QK) + EPS)
        kh_ref[h, :, :QK_NOPE] = (kn * r * kgn_ref[...]).astype(kh_ref.dtype)
        kh_ref[h, :, QK_NOPE:] = (kr_rot * r).astype(kh_ref.dtype)
        vh_ref[h] = kv_ref[:, (heads + h) * HEAD_DIM:(heads + h + 1) * HEAD_DIM].astype(vh_ref.dtype)


def _mla_post(q_raw, kv_raw, pb, kr_block, qn_gain, kn_gain, cos_t, sin_t, rows, heads):
    m = q_raw.shape[0]
    bm = rows.block(ROW_BLOCK)
    n_lat, n_tab = rows.m_lat // bm, rows.seq // bm
    tab = lambda i: (jnp.where(i < n_lat, i % n_tab, n_tab), 0)
    pair = lambda g: jnp.concatenate([g, g]).reshape(1, LANE)
    return pl.pallas_call(
        functools.partial(_mla_post_kernel, heads=heads),
        grid=(m // bm,),
        in_specs=[pl.BlockSpec((bm, q_raw.shape[1]), lambda i: (i, 0)),
                  pl.BlockSpec((bm, kv_raw.shape[1]), lambda i: (i, 0)),
                  pl.BlockSpec((bm, LANE), lambda i: (i, kr_block)),
                  pl.BlockSpec((1, QK_NOPE), lambda i: (0, 0)),
                  pl.BlockSpec((1, LANE), lambda i: (0, 0)),
                  pl.BlockSpec((1, QK_NOPE), lambda i: (0, 0)),
                  pl.BlockSpec((1, LANE), lambda i: (0, 0)),
                  pl.BlockSpec((bm, LANE), tab),
                  pl.BlockSpec((bm, LANE), tab)],
        out_specs=[pl.BlockSpec((heads, bm, B_QK_PAD), lambda i: (0, i, 0)),
                   pl.BlockSpec((heads, bm, B_QK_PAD), lambda i: (0, i, 0)),
                   pl.BlockSpec((heads, bm, HEAD_DIM), lambda i: (0, i, 0))],
        out_shape=[jax.ShapeDtypeStruct((heads, m, B_QK_PAD), BF16),
                   jax.ShapeDtypeStruct((heads, m, B_QK_PAD), BF16),
                   jax.ShapeDtypeStruct((heads, m, HEAD_DIM), BF16)],
        compiler_params=_cparams(("parallel",)),
        name="mla_post",
    )(q_raw, kv_raw, pb, qn_gain[:QK_NOPE].reshape(1, QK_NOPE), pair(qn_gain[QK_NOPE:]),
      kn_gain[:QK_NOPE].reshape(1, QK_NOPE), pair(kn_gain[QK_NOPE:]), cos_t, sin_t)


def _mla_attn_kernel(q_ref, kc_ref, vc_ref, *rest, with_lat):
    if with_lat:
        kl_ref, vl_ref, o_ref = rest
    else:
        (o_ref,) = rest
    c = (B_QK ** -0.5) * LOG2E
    tn = (((0,), (0,)), ((), ()))
    q = q_ref[...]
    s_c = _dot_nt(kc_ref[...], q)
    m = jnp.max(s_c, axis=0, keepdims=True)
    if with_lat:
        s_l = _dot_nt(kl_ref[...], q)
        m = jnp.maximum(m, jnp.max(s_l, axis=0, keepdims=True))
    p_c = jnp.exp2((s_c - m) * c)
    den = jnp.sum(p_c, axis=0, keepdims=True)
    acc = lax.dot_general(vc_ref[...], p_c.astype(BF16), tn, preferred_element_type=F32)
    if with_lat:
        p_l = jnp.exp2((s_l - m) * c)
        den = den + jnp.sum(p_l, axis=0, keepdims=True)
        acc = acc + lax.dot_general(vl_ref[...], p_l.astype(BF16), tn, preferred_element_type=F32)
    o_ref[...] = (acc / den).T.astype(o_ref.dtype)


def _mla_attention(qh, kh, vh, rows, heads, latent_queries):
    b, s, c = rows.batch, rows.seq, rows.ctx_len
    ctx_blk0 = rows.m_lat // c
    if latent_queries:
        tq = _pick(s, 2048, 8)
        nq = s // tq
        q_map = lambda bi, h, i: (h, bi * nq + i, 0)
        o_map = lambda bi, h, i: (bi * nq + i, h)
        m_out = rows.m_lat
    else:
        tq, nq = c, 1
        q_map = lambda bi, h, i: (h, ctx_blk0 + bi, 0)
        o_map = lambda bi, h, i: (bi, h)
        m_out = rows.m_ctx
    in_specs = [pl.BlockSpec((None, tq, B_QK_PAD), q_map),
                pl.BlockSpec((None, c, B_QK_PAD), lambda bi, h, i: (h, ctx_blk0 + bi, 0)),
                pl.BlockSpec((None, c, HEAD_DIM), lambda bi, h, i: (h, ctx_blk0 + bi, 0))]
    args = [qh, kh, vh]
    if latent_queries:
        in_specs += [pl.BlockSpec((None, s, B_QK_PAD), lambda bi, h, i: (h, bi, 0)),
                     pl.BlockSpec((None, s, HEAD_DIM), lambda bi, h, i: (h, bi, 0))]
        args += [kh, vh]
    return pl.pallas_call(
        functools.partial(_mla_attn_kernel, with_lat=latent_queries),
        grid=(b, heads, nq),
        in_specs=in_specs,
        out_specs=pl.BlockSpec((tq, HEAD_DIM), o_map),
        out_shape=jax.ShapeDtypeStruct((m_out, heads * HEAD_DIM), BF16),
        compiler_params=_cparams(("parallel", "parallel", "parallel")),
        name="mla_attention",
    )(*args)


def _gqa_post_kernel(p_ref, qg_ref, kg_ref, cos_ref, sin_ref, q_ref, k_ref, v_ref, *, q_heads, kv_heads):
    cos, sin = cos_ref[...], sin_ref[...]
    half = HEAD_DIM // 4

    def norm_rope(x, gain):
        xn = x * lax.rsqrt(jnp.mean(x * x, axis=-1, keepdims=True) + EPS) * gain
        return _rope(xn, cos, sin, half)

    for h in range(q_heads):
        sl = slice(h * HEAD_DIM, (h + 1) * HEAD_DIM)
        q_ref[:, sl] = norm_rope(p_ref[:, sl], qg_ref[...]).astype(q_ref.dtype)
    for h in range(kv_heads):
        sl = slice(h * HEAD_DIM, (h + 1) * HEAD_DIM)
        k0 = q_heads * HEAD_DIM
        v0 = (q_heads + kv_heads) * HEAD_DIM
        k_ref[:, sl] = norm_rope(p_ref[:, k0 + h * HEAD_DIM:k0 + (h + 1) * HEAD_DIM],
                                 kg_ref[...]).astype(k_ref.dtype)
        v_ref[:, sl] = p_ref[:, v0 + h * HEAD_DIM:v0 + (h + 1) * HEAD_DIM].astype(v_ref.dtype)


def _gqa_post(pc, qn_gain, kn_gain, cos_t, sin_t, rows, q_heads, kv_heads):
    m, n = pc.shape
    bm = rows.block(ROW_BLOCK)
    n_lat, n_tab = rows.m_lat // bm, rows.seq // bm
    tab = lambda i: (jnp.where(i < n_lat, i % n_tab, n_tab), 0)
    return pl.pallas_call(
        functools.partial(_gqa_post_kernel, q_heads=q_heads, kv_heads=kv_heads),
        grid=(m // bm,),
        in_specs=[pl.BlockSpec((bm, n), lambda i: (i, 0)),
                  pl.BlockSpec((1, HEAD_DIM), lambda i: (0, 0)),
                  pl.BlockSpec((1, HEAD_DIM), lambda i: (0, 0)),
                  pl.BlockSpec((bm, LANE), tab),
                  pl.BlockSpec((bm, LANE), tab)],
        out_specs=[pl.BlockSpec((bm, q_heads * HEAD_DIM), lambda i: (i, 0)),
                   pl.BlockSpec((bm, kv_heads * HEAD_DIM), lambda i: (i, 0)),
                   pl.BlockSpec((bm, kv_heads * HEAD_DIM), lambda i: (i, 0))],
        out_shape=[jax.ShapeDtypeStruct((m, q_heads * HEAD_DIM), BF16),
                   jax.ShapeDtypeStruct((m, kv_heads * HEAD_DIM), BF16),
                   jax.ShapeDtypeStruct((m, kv_heads * HEAD_DIM), BF16)],
        compiler_params=_cparams(("parallel",)),
        name="gqa_post",
    )(pc, qn_gain.reshape(1, HEAD_DIM), kn_gain.reshape(1, HEAD_DIM), cos_t, sin_t)


def _gqa_attn_kernel(sink_ref, q_ref, kc_ref, vc_ref, *rest, group, seq, banded):
    if banded:
        kp_ref, kq_ref, kn_ref, vp_ref, vq_ref, vn_ref, o_ref = rest
    else:
        (o_ref,) = rest
    scale = HEAD_DIM ** -0.5
    blk = pl.program_id(1)
    rows_q = group * WINDOW
    kv_heads = kc_ref.shape[1] // HEAD_DIM
    row = lax.broadcasted_iota(jnp.int32, (rows_q, 1), 0)

    for kvh in range(kv_heads):
        hs = slice(kvh * HEAD_DIM, (kvh + 1) * HEAD_DIM)
        q0 = kvh * group * HEAD_DIM
        q = jnp.concatenate([q_ref[:, q0 + g * HEAD_DIM:q0 + (g + 1) * HEAD_DIM]
                             for g in range(group)], axis=0)
        sink = jnp.zeros((rows_q, 1), F32)
        for g in range(group):
            in_g = (row >= g * WINDOW) & (row < (g + 1) * WINDOW)
            sink = jnp.where(in_g, sink_ref[kvh * group + g], sink)

        s_c = _dot_nt(q, kc_ref[:, hs]) * scale
        m = jnp.maximum(jnp.max(s_c, axis=-1, keepdims=True), sink)
        if banded:
            kb = jnp.concatenate([kp_ref[:, hs], kq_ref[:, hs], kn_ref[:, hs]], axis=0)
            vb = jnp.concatenate([vp_ref[:, hs], vq_ref[:, hs], vn_ref[:, hs]], axis=0)
            s_b = _dot_nt(q, kb) * scale
            qpos = lax.broadcasted_iota(jnp.int32, s_b.shape, 0) & (WINDOW - 1)
            krel = lax.broadcasted_iota(jnp.int32, s_b.shape, 1) - WINDOW
            kpos = blk * WINDOW + krel
            valid = (jnp.abs(qpos - krel) <= WINDOW) & (kpos >= 0) & (kpos < seq)
            s_b = jnp.where(valid, s_b, NEG)
            m = jnp.maximum(m, jnp.max(s_b, axis=-1, keepdims=True))
        p_c = jnp.exp(s_c - m)
        den = jnp.sum(p_c, axis=-1, keepdims=True) + jnp.exp(sink - m)
        acc = jnp.dot(p_c.astype(BF16), vc_ref[:, hs], preferred_element_type=F32)
        if banded:
            p_b = jnp.exp(s_b - m)
            den = den + jnp.sum(p_b, axis=-1, keepdims=True)
            acc = acc + jnp.dot(p_b.astype(BF16), vb, preferred_element_type=F32)
        o = acc / den
        for g in range(group):
            o_ref[:, q0 + g * HEAD_DIM:q0 + (g + 1) * HEAD_DIM] = (
                o[g * WINDOW:(g + 1) * WINDOW].astype(o_ref.dtype))


def _gqa_attention(qc, kc, vc, sink, rows, q_heads, kv_heads, latent_queries):
    b, s, c = rows.batch, rows.seq, rows.ctx_len
    group = q_heads // kv_heads
    qw, kw = q_heads * HEAD_DIM, kv_heads * HEAD_DIM
    ctx_blk0 = rows.m_lat // c
    ctx_spec = lambda: pl.BlockSpec((c, kw), lambda bi, n: (ctx_blk0 + bi, 0))
    if latent_queries:
        nb = s // WINDOW
        qrow = lambda bi, n: bi * nb + n
        m_out = rows.m_lat
        orow = qrow
    else:
        nb = c // WINDOW
        qrow = lambda bi, n: rows.m_lat // WINDOW + bi * nb + n
        orow = lambda bi, n: bi * nb + n
        m_out = rows.m_ctx
    in_specs = [pl.BlockSpec(memory_space=pltpu.SMEM),
                pl.BlockSpec((WINDOW, qw), lambda bi, n: (qrow(bi, n), 0)),
                ctx_spec(), ctx_spec()]
    args = [sink, qc, kc, vc]
    if latent_queries:
        band = [lambda bi, n: (bi * nb + jnp.maximum(n - 1, 0), 0),
                lambda bi, n: (bi * nb + n, 0),
                lambda bi, n: (bi * nb + jnp.minimum(n + 1, nb - 1), 0)]
        in_specs += [pl.BlockSpec((WINDOW, kw), f) for f in band] * 2
        args += [kc, kc, kc, vc, vc, vc]
    return pl.pallas_call(
        functools.partial(_gqa_attn_kernel, group=group, seq=s, banded=latent_queries),
        grid=(b, nb),
        in_specs=in_specs,
        out_specs=pl.BlockSpec((WINDOW, qw), lambda bi, n: (orow(bi, n), 0)),
        out_shape=jax.ShapeDtypeStruct((m_out, qw), BF16),
        compiler_params=_cparams(("parallel", "parallel")),
        name="gqa_attention",
    )(*args)


def _extract_topk(s, pos, ids, count, val_ref, id_ref):
    beyond = 3.0e38
    for r in range(count):
        top = jnp.max(s, axis=0, keepdims=True)
        first = jnp.min(jnp.where(s == top, pos, beyond), axis=0, keepdims=True)
        hit = pos == first
        val_ref[r:r + 1, :] = top
        if ids is None:
            id_ref[r:r + 1, :] = first
        else:
            id_ref[r:r + 1, :] = jnp.max(jnp.where(hit, ids, -1.0), axis=0, keepdims=True)
        s = jnp.where(hit, -jnp.inf, s)


_PAIR_SPANS = [(a, PEER_TOPK // (a + 1)) for a in range(PEER_TOPK)]
_N_PAIRS = sum(n for _, n in _PAIR_SPANS)
_N_PAIRS_PAD = -(-_N_PAIRS // 8) * 8


def _route_kernel(q_ref, sk_ref, idx_ref, gate_ref, tv_ref, ti_ref, cs_ref, cp_ref, ci_ref, bs_ref,
                  bi_ref, *, n_keys):
    tt = q_ref.shape[0]
    key_id = lax.broadcasted_iota(jnp.int32, (n_keys, tt), 0).astype(F32)
    for p in range(2):
        qp = q_ref[:, p * LANE:(p + 1) * LANE].astype(BF16)
        scores = _dot_nt(sk_ref[p], qp)
        _extract_topk(scores, key_id, None, PEER_TOPK, tv_ref.at[p], ti_ref.at[p])
    cs_ref[...] = jnp.full(cs_ref.shape, -jnp.inf, F32)
    cp_ref[...] = jnp.full(cp_ref.shape, 1.0e6, F32)
    ci_ref[...] = jnp.full(ci_ref.shape, -1.0, F32)
    off = 0
    for a, n in _PAIR_SPANS:
        rs = slice(off, off + n)
        cs_ref[rs, :] = tv_ref[0, a:a + 1, :] + tv_ref[1, 0:n, :]
        cp_ref[rs, :] = float(a * PEER_TOPK) + lax.broadcasted_iota(jnp.int32, (n, tt), 0).astype(F32)
        ci_ref[rs, :] = ti_ref[0, a:a + 1, :] * float(n_keys) + ti_ref[1, 0:n, :]
        off += n
    _extract_topk(cs_ref[...], cp_ref[...], ci_ref[...], PEER_TOPK, bs_ref, bi_ref)
    best = bs_ref[...]
    e = jnp.exp(best - best[0:1, :])
    gate_ref[...] = e / jnp.sum(e, axis=0, keepdims=True)
    idx_ref[...] = bi_ref[...].astype(jnp.int32)


def _peer_route(pq, subkeys):
    m = pq.shape[0]
    heads, _, n_keys, half = subkeys.shape
    assert half == LANE
    tt = _pick(m, 1024)
    k = PEER_TOPK
    return pl.pallas_call(
        functools.partial(_route_kernel, n_keys=n_keys),
        grid=(m // tt, heads),
        in_specs=[pl.BlockSpec((tt, 2 * LANE), lambda i, h: (i, h)),
                  pl.BlockSpec((None, 2, n_keys, half), lambda i, h: (h, 0, 0, 0))],
        out_specs=[pl.BlockSpec((None, k, tt), lambda i, h: (h, 0, i)),
                   pl.BlockSpec((None, k, tt), lambda i, h: (h, 0, i))],
        out_shape=[jax.ShapeDtypeStruct((heads, k, m), jnp.int32),
                   jax.ShapeDtypeStruct((heads, k, m), F32)],
        scratch_shapes=[pltpu.VMEM((2, k, tt), F32), pltpu.VMEM((2, k, tt), F32)]
        + [pltpu.VMEM((_N_PAIRS_PAD, tt), F32)] * 3
        + [pltpu.VMEM((k, tt), F32), pltpu.VMEM((k, tt), F32)],
        compiler_params=_cparams(("parallel", "parallel")),
        name="peer_route",
    )(pq, subkeys.astype(BF16))


HI16 = 0xFFFF0000


def _pack_kernel(u_ref, v_ref, o_ref):
    ub = lax.bitcast_convert_type(u_ref[...].astype(BF16).astype(F32), jnp.uint32)
    vb = lax.bitcast_convert_type(v_ref[...].astype(BF16).astype(F32), jnp.uint32)
    o_ref[...] = (vb & jnp.uint32(HI16)) | (ub >> 16)


def _pack_tables(u_all, v_all, layer):
    _, e, d = u_all.shape
    be = _pick(e, ROW_BLOCK, 8)
    spec = pl.BlockSpec((None, be, d), lambda i: (layer, i, 0))
    return pl.pallas_call(
        _pack_kernel,
        grid=(e // be,),
        in_specs=[spec, spec],
        out_specs=pl.BlockSpec((be, d), lambda i: (i, 0)),
        out_shape=jax.ShapeDtypeStruct((e, d), jnp.uint32),
        compiler_params=_cparams(("parallel",)),
        name="peer_pack",
    )(u_all, v_all)


PEER_RING = 8
PEER_AHEAD = PEER_RING - 1


def _peer_kernel(idx_ref, g_ref, gate_ref, tab_hbm, o_ref, *scratch, tokens, n_sel):
    bufs, sem = scratch[:PEER_RING], scratch[PEER_RING]
    step = pl.program_id(0)
    tile = SUBLANE

    def row_copy(e, s, k):
        return pltpu.make_async_copy(tab_hbm.at[pl.ds(e, 1), :], bufs[s].at[pl.ds(k, 1), :], sem.at[s])

    def wait(s):
        pltpu.make_async_copy(tab_hbm.at[pl.ds(0, n_sel), :], bufs[s], sem.at[s]).wait()

    @pl.when(step == 0)
    def _():
        for t in range(PEER_AHEAD):
            for k in range(n_sel):
                row_copy(idx_ref[t, k], t, k).start()

    eye = (lax.broadcasted_iota(jnp.int32, (n_sel, n_sel), 0)
           == lax.broadcasted_iota(jnp.int32, (n_sel, n_sel), 1))

    def process(t, s):
        cur, s_nxt = bufs[s], (s + PEER_AHEAD) % PEER_RING
        wait(s)
        g = g_ref[pl.ds(t, 1), :]
        parts = []
        for c in range(n_sel // tile):
            for k in range(c * tile, (c + 1) * tile):
                row_copy(idx_ref[t + PEER_AHEAD, k], s_nxt, k).start()
            u = lax.bitcast_convert_type(cur[c * tile:(c + 1) * tile, :] << 16, F32)
            parts.append(jnp.sum(u * g, axis=1, keepdims=True))
        d = jnp.concatenate(parts, axis=0)
        gate_row = gate_ref[pl.ds(t, 1), :]
        gate_col = jnp.sum(jnp.where(eye, gate_row, 0.0), axis=1, keepdims=True)
        w = gate_col * _gelu(d)
        acc = jnp.zeros((tile, g.shape[1]), F32)
        for c in range(n_sel // tile):
            v = lax.bitcast_convert_type(cur[c * tile:(c + 1) * tile, :] & jnp.uint32(HI16), F32)
            acc = acc + v * w[c * tile:(c + 1) * tile, :]
        o_ref[pl.ds(t, 1), :] = jnp.sum(acc, axis=0, keepdims=True)

    def ring(i, carry):
        for s in range(PEER_RING):
            process(PEER_RING * i + s, s)
        return carry

    lax.fori_loop(0, tokens // PEER_RING, ring, 0)

    @pl.when(step == pl.num_programs(0) - 1)
    def _():
        for s in range(PEER_AHEAD):
            wait(s)


def _peer_experts(idx, gate, g, tab):
    m, d = g.shape
    n_sel = idx.shape[1]
    tb = _pick(m, 128, 8)
    nb = m // tb
    assert tb % PEER_RING == 0 and PEER_AHEAD <= SUBLANE
    blocks = idx.reshape(nb, tb, n_sel)
    idx_ext = jnp.concatenate([blocks, jnp.roll(blocks[:, :SUBLANE], -1, axis=0)], axis=1)
    return pl.pallas_call(
        functools.partial(_peer_kernel, tokens=tb, n_sel=n_sel),
        grid=(nb,),
        in_specs=[pl.BlockSpec((None, tb + SUBLANE, n_sel), lambda i: (i, 0, 0),
                               memory_space=pltpu.SMEM),
                  pl.BlockSpec((tb, d), lambda i: (i, 0)),
                  pl.BlockSpec((tb, n_sel), lambda i: (i, 0)),
                  pl.BlockSpec(memory_space=pl.ANY)],
        out_specs=pl.BlockSpec((tb, d), lambda i: (i, 0)),
        out_shape=jax.ShapeDtypeStruct((m, d), F32),
        scratch_shapes=[pltpu.VMEM((n_sel, d), jnp.uint32)] * PEER_RING
        + [pltpu.SemaphoreType.DMA((PEER_RING,))],
        compiler_params=_cparams(("arbitrary",)),
        name="peer_experts",
    )(idx_ext, g, gate, tab)


def _rope_tables(seq, rot_dim, pad):
    rows = seq // GRID_W
    row = jnp.repeat(jnp.arange(rows, dtype=F32), GRID_W)
    col = jnp.tile(jnp.arange(GRID_W, dtype=F32), rows)
    n_freq = rot_dim // 4
    inv = ROPE_BASE ** (-jnp.arange(n_freq, dtype=F32) / n_freq)
    ar, ac = row[:, None] * inv, col[:, None] * inv
    ang = jnp.concatenate([ar, ar, ac, ac], axis=-1)
    sign = jnp.concatenate([-jnp.ones(n_freq), jnp.ones(n_freq)] * 2).astype(F32)
    cos, sin = jnp.cos(ang), jnp.sin(ang) * sign
    reps = LANE // rot_dim
    cos, sin = jnp.tile(cos, (1, reps)), jnp.tile(sin, (1, reps))
    cos = jnp.concatenate([cos, jnp.ones((pad, LANE), F32)], axis=0)
    sin = jnp.concatenate([sin, jnp.zeros((pad, LANE), F32)], axis=0)
    return cos, sin


def kernel(x, c, ctx, c_ctx, w_mod, b_mod, norm1_gain, norm2_gain, w_in, a_v_gain, a_w_s, a_b_s,
           b_q_gain, b_kv_gain, b_w_uq, b_w_ukv, b_qn_gain, b_kn_gain, c_qn_gain, c_kn_gain, c_sink,
           w_out, peer_w_q, peer_subkeys, peer_u, peer_v):
    batch, seq, d = x.shape
    ctx_len = ctx.shape[1]
    depth = w_mod.shape[0]
    a_width = a_v_gain.shape[1]
    q_lora, kv_lora = b_q_gain.shape[1], b_kv_gain.shape[1]
    b_heads = b_w_uq.shape[2] // B_QK
    c_heads = c_sink.shape[1]
    c_kv = c_heads // 3
    p_a = 2 * a_width
    p_b = q_lora + kv_lora + QK_ROPE
    p_b_pad = -(-p_b // (2 * LANE)) * (2 * LANE)
    assert b_heads % 2 == 0 and (q_lora + kv_lora) % LANE == 0
    assert seq % WINDOW == 0 and ctx_len % WINDOW == 0

    rows_all = _Rows(batch, seq, ctx_len)
    bm_tab = rows_all.block(ROW_BLOCK)
    cos_b, sin_b = _rope_tables(seq, QK_ROPE, bm_tab)
    cos_c, sin_c = _rope_tables(seq, HEAD_DIM, bm_tab)

    c8 = jnp.zeros((8, d), F32).at[:batch].set(c).at[batch].set(c_ctx)
    xs = jnp.concatenate([x.reshape(batch * seq, d), ctx.reshape(batch * ctx_len, d)], axis=0)

    for layer in range(depth):
        need_ctx = layer < depth - 1
        mod3 = _modulation(c8, w_mod, b_mod[layer], layer).reshape(8, 1, N_MOD * d)

        w_l = w_in[layer]
        w_a = w_l[:, :p_a].astype(BF16)
        w_b = jnp.pad(w_l[:, p_a:p_a + p_b], ((0, 0), (0, p_b_pad - p_b))).astype(BF16)
        w_c = w_l[:, p_a + p_b:].astype(BF16)
        w_uq = b_w_uq[layer].reshape(q_lora, b_heads, B_QK)
        w_uq = jnp.concatenate([w_uq[:, :, :QK_NOPE].reshape(q_lora, -1),
                                w_uq[:, :, QK_NOPE:].reshape(q_lora, -1)], axis=1).astype(BF16)
        w_ukv = b_w_ukv[layer].reshape(kv_lora, b_heads, QK_NOPE + HEAD_DIM)
        w_ukv = jnp.concatenate([w_ukv[:, :, :QK_NOPE].reshape(kv_lora, -1),
                                 w_ukv[:, :, QK_NOPE:].reshape(kv_lora, -1)], axis=1).astype(BF16)

        (h,) = _norm_mod(xs, norm1_gain[layer], mod3, 0, 1, rows_all, [BF16])
        pa = _matmul(h, w_a)
        pb = _matmul(h, w_b, bn_target=1152)
        pc = _matmul(h, w_c, bn_target=1280)

        oa = _gmlp(pa, a_v_gain[layer], a_w_s[layer], a_b_s[layer])

        cq, ckv = _mla_pre(pb, b_q_gain[layer], b_kv_gain[layer], rows_all)
        q_raw = _matmul(cq, w_uq, bm_target=2304, bn_target=1152)
        kv_raw = _matmul(ckv, w_ukv, bm_target=3072)
        qh, kh, vh = _mla_post(q_raw, kv_raw, pb, (q_lora + kv_lora) // LANE, b_qn_gain[layer],
                               b_kn_gain[layer], cos_b, sin_b, rows_all, b_heads)
        ob = _mla_attention(qh, kh, vh, rows_all, b_heads, True)

        qc, kc, vc = _gqa_post(pc, c_qn_gain[layer], c_kn_gain[layer], cos_c, sin_c, rows_all,
                               c_heads, c_kv)
        oc = _gqa_attention(qc, kc, vc, c_sink[layer], rows_all, c_heads, c_kv, True)

        if need_ctx:
            rows = rows_all
            ob = jnp.concatenate([ob, _mla_attention(qh, kh, vh, rows_all, b_heads, False)], axis=0)
            oc = jnp.concatenate(
                [oc, _gqa_attention(qc, kc, vc, c_sink[layer], rows_all, c_heads, c_kv, False)], axis=0)
        else:
            rows = _Rows(batch, seq, 0)
        w_o = w_out[layer].astype(BF16)
        n_b = b_heads * HEAD_DIM
        w_parts = [w_o[:a_width], w_o[a_width:a_width + n_b], w_o[a_width + n_b:]]
        xs = _matmul_residual([oa, ob, oc], w_parts, xs, mod3, 2, rows)

        g_bf, g = _norm_mod(xs, norm2_gain[layer], mod3, 3, 4, rows, [BF16, F32])
        pq = _matmul(g_bf, peer_w_q[layer].astype(BF16))
        idx, gate = _peer_route(pq, peer_subkeys[layer])
        n_sel = idx.shape[0] * idx.shape[1]
        idx = idx.reshape(n_sel, -1).T
        gate = gate.reshape(n_sel, -1).T
        y = _peer_experts(idx, gate, g, _pack_tables(peer_u, peer_v, layer))
        xs = _gated_add(xs, y, mod3, 5, rows)

    return xs[:batch * seq].reshape(batch, seq, d)
```
